```python
import math
import jax, jax.numpy as jnp
from jax import lax
import numpy as np

D_MODEL = 2048
BATCH = 2
SEQ = 8192
DEPTH = 1

EPS = 1e-6
ROPE_THETA = 10000.0
Q_BLOCK = 128

MLA_HEADS = 8
MLA_Q_RANK = 768
MLA_KV_RANK = 512
MLA_NOPE = 128
MLA_ROPE = 64
MLA_V = 128
MLA_QK = MLA_NOPE + MLA_ROPE

DIFF_HEADS = 8
DIFF_HD = 64
DIFF_VD = 2 * DIFF_HD

PEER_HEADS = 8
PEER_NKEYS = 128
PEER_EXPERTS = PEER_NKEYS * PEER_NKEYS
PEER_QDIM = 256
PEER_HALF = PEER_QDIM // 2
PEER_TOPK = 16
PEER_CHUNK = 128

OFF_CQ = 0
OFF_CKV = OFF_CQ + MLA_Q_RANK
OFF_KR = OFF_CKV + MLA_KV_RANK
OFF_DQ = OFF_KR + MLA_ROPE
OFF_DK = OFF_DQ + DIFF_HEADS * 2 * DIFF_HD
OFF_DV = OFF_DK + DIFF_HEADS * 2 * DIFF_HD
OFF_GATE = OFF_DV + DIFF_HEADS * DIFF_VD
IN_COLS = OFF_GATE + 2 * D_MODEL

kernel_name = "hybrid_mla_diffattn_peer"


def rmsnorm(x, g):
    xf = x.astype(jnp.float32)
    y = xf * lax.rsqrt(jnp.mean(xf * xf, axis=-1, keepdims=True) + EPS)
    return (y * g.astype(jnp.float32)).astype(x.dtype)


def rope(x, pos):
    d = x.shape[-1]
    inv = ROPE_THETA ** (-jnp.arange(0, d, 2, dtype=jnp.float32) / d)
    ang = pos.astype(jnp.float32)[:, None] * inv[None, :]
    shape = (1, pos.shape[0]) + (1,) * (x.ndim - 3) + (d,)
    cos = jnp.concatenate([jnp.cos(ang), jnp.cos(ang)], -1).reshape(shape)
    sin = jnp.concatenate([jnp.sin(ang), jnp.sin(ang)], -1).reshape(shape)
    xf = x.astype(jnp.float32)
    x1, x2 = xf[..., : d // 2], xf[..., d // 2:]
    rot = jnp.concatenate([-x2, x1], -1)
    return (xf * cos + rot * sin).astype(x.dtype)


def _over_query_blocks(fn, qs):
    b, s = qs[0].shape[:2]
    nb = s // Q_BLOCK
    split = lambda a: a.reshape(b, nb, Q_BLOCK, *a.shape[2:]).swapaxes(0, 1)
    out = lax.map(lambda args: fn(args[0], *args[1:]),
                  (jnp.arange(nb), *[split(a) for a in qs]))
    return out.swapaxes(0, 1).reshape(b, s, *out.shape[3:])


def _causal_probs(q, k, blk, scale):
    qb, s = q.shape[1], k.shape[1]
    sc = jnp.einsum('bqhd,bkhd->bhqk', q, k).astype(jnp.float32) * scale
    qpos = blk * qb + jnp.arange(qb)
    mask = jnp.arange(s)[None, :] <= qpos[:, None]
    sc = jnp.where(mask, sc, -jnp.inf)
    return jax.nn.softmax(sc, axis=-1)


def mla_branch(proj, g_cq, w_uq, g_ckv, w_ukv, pos):
    b, s, _ = proj.shape
    c_q = rmsnorm(proj[..., OFF_CQ:OFF_CKV], g_cq)
    c_kv = rmsnorm(proj[..., OFF_CKV:OFF_KR], g_ckv)
    k_rope = rope(proj[..., OFF_KR:OFF_DQ][:, :, None, :], pos)
    q = (c_q @ w_uq).reshape(b, s, MLA_HEADS, MLA_QK)
    q = jnp.concatenate([q[..., :MLA_NOPE], rope(q[..., MLA_NOPE:], pos)], -1)
    kv = (c_kv @ w_ukv).reshape(b, s, MLA_HEADS, MLA_NOPE + MLA_V)
    k = jnp.concatenate([kv[..., :MLA_NOPE],
                         jnp.broadcast_to(k_rope, (b, s, MLA_HEADS, MLA_ROPE))], -1)
    v = kv[..., MLA_NOPE:]
    scale = MLA_QK ** -0.5

    def attend(blk, q_blk):
        p = _causal_probs(q_blk, k, blk, scale)
        return jnp.einsum('bhqk,bkhd->bqhd', p.astype(v.dtype), v)

    o = _over_query_blocks(attend, [q])
    return o.reshape(b, s, MLA_HEADS * MLA_V)


def diff_branch(proj, lambda_qk, g_subln, lambda_init, pos):
    b, s, _ = proj.shape
    q = rope(proj[..., OFF_DQ:OFF_DK].reshape(b, s, DIFF_HEADS, 2, DIFF_HD), pos)
    k = rope(proj[..., OFF_DK:OFF_DV].reshape(b, s, DIFF_HEADS, 2, DIFF_HD), pos)
    v = proj[..., OFF_DV:OFF_GATE].reshape(b, s, DIFF_HEADS, DIFF_VD)
    q1, q2 = q[..., 0, :], q[..., 1, :]
    k1, k2 = k[..., 0, :], k[..., 1, :]
    lq = lambda_qk.astype(jnp.float32)
    lam = (jnp.exp(jnp.sum(lq[0] * lq[1])) - jnp.exp(jnp.sum(lq[2] * lq[3]))
           + lambda_init)
    scale = DIFF_HD ** -0.5

    def attend(blk, q1b, q2b):
        p = _causal_probs(q1b, k1, blk, scale) - lam * _causal_probs(q2b, k2, blk, scale)
        return jnp.einsum('bhqk,bkhd->bqhd', p.astype(v.dtype), v)

    o = _over_query_blocks(attend, [q1, q2])
    o = rmsnorm(o, g_subln) * (1.0 - lambda_init)
    return o.reshape(b, s, DIFF_HEADS * DIFF_VD)


def peer_ffn(h, w_q, sub_keys, expert_u, expert_v):
    b, s, d = h.shape
    t = b * s
    hf = h.reshape(t, d)
    q = (hf @ w_q).reshape(t, PEER_HEADS, 2, PEER_HALF).astype(jnp.float32)
    sk = sub_keys.astype(jnp.float32)
    s1 = jnp.einsum('thd,nd->thn', q[:, :, 0], sk[0])
    s2 = jnp.einsum('thd,nd->thn', q[:, :, 1], sk[1])
    v1, i1 = lax.top_k(s1, PEER_TOPK)
    v2, i2 = lax.top_k(s2, PEER_TOPK)
    cand = (v1[..., :, None] + v2[..., None, :]).reshape(t, PEER_HEADS, PEER_TOPK * PEER_TOPK)
    cs, ci = lax.top_k(cand, PEER_TOPK)
    row = jnp.take_along_axis(i1, ci // PEER_TOPK, axis=-1)
    col = jnp.take_along_axis(i2, ci % PEER_TOPK, axis=-1)
    experts = row * PEER_NKEYS + col
    gates = jax.nn.softmax(cs, axis=-1)
    nc = t // PEER_CHUNK

    def mix(args):
        hc, ec, gc = args
        a = jax.nn.gelu(jnp.einsum('chkd,cd->chk', expert_u[ec], hc).astype(jnp.float32))
        w = (gc * a).astype(hc.dtype)
        return jnp.einsum('chk,chkd->cd', w, expert_v[ec])

    out = lax.map(mix, (hf.reshape(nc, PEER_CHUNK, d),
                        experts.reshape(nc, PEER_CHUNK, PEER_HEADS, PEER_TOPK),
                        gates.reshape(nc, PEER_CHUNK, PEER_HEADS, PEER_TOPK)))
    return out.reshape(b, s, d)


def setup_inputs(seed: int = 0) -> dict:
    key = jax.random.key(seed)
    ks = jax.random.split(key, 20)
    f32 = jnp.float32
    nrm = lambda k, shape, sc: jax.random.normal(k, shape, f32) * sc
    gain = lambda k, shape: 1.0 + 0.02 * jax.random.normal(k, shape, f32)
    return {
        "x": jax.random.normal(ks[0], (BATCH, SEQ, D_MODEL), f32),
        "w_in": nrm(ks[1], (DEPTH, D_MODEL, IN_COLS), D_MODEL ** -0.5),
        "b_gate": nrm(ks[2], (DEPTH, 2 * D_MODEL), 0.02),
        "g_norm1": gain(ks[3], (DEPTH, D_MODEL)),
        "g_cq": gain(ks[4], (DEPTH, MLA_Q_RANK)),
        "w_uq": nrm(ks[5], (DEPTH, MLA_Q_RANK, MLA_HEADS * MLA_QK), MLA_Q_RANK ** -0.5),
        "g_ckv": gain(ks[6], (DEPTH, MLA_KV_RANK)),
        "w_ukv": nrm(ks[7], (DEPTH, MLA_KV_RANK, MLA_HEADS * (MLA_NOPE + MLA_V)), MLA_KV_RANK ** -0.5),
        "w_o_mla": nrm(ks[8], (DEPTH, MLA_HEADS * MLA_V, D_MODEL), (MLA_HEADS * MLA_V) ** -0.5),
        "lambda_qk": nrm(ks[9], (DEPTH, 4, DIFF_HD), 0.1),
        "g_subln": gain(ks[10], (DEPTH, DIFF_VD)),
        "w_o_diff": nrm(ks[11], (DEPTH, DIFF_HEADS * DIFF_VD, D_MODEL), (DIFF_HEADS * DIFF_VD) ** -0.5),
        "w_out": nrm(ks[12], (DEPTH, D_MODEL, D_MODEL), D_MODEL ** -0.5),
        "g_norm2": gain(ks[13], (DEPTH, D_MODEL)),
        "w_q_peer": nrm(ks[14], (DEPTH, D_MODEL, PEER_HEADS * PEER_QDIM), D_MODEL ** -0.5),
        "sub_keys": nrm(ks[15], (DEPTH, 2, PEER_NKEYS, PEER_HALF), PEER_HALF ** -0.5),
        "expert_u": nrm(ks[16], (DEPTH, PEER_EXPERTS, D_MODEL), D_MODEL ** -0.5),
        "expert_v": nrm(ks[17], (DEPTH, PEER_EXPERTS, D_MODEL), PEER_HEADS ** -0.5),
        "g_final": gain(ks[18], (D_MODEL,)),
    }


def reference(x, w_in, b_gate, g_norm1, g_cq, w_uq, g_ckv, w_ukv, w_o_mla, lambda_qk,
              g_subln, w_o_diff, w_out, g_norm2, w_q_peer, sub_keys, expert_u, expert_v,
              g_final):
    pos = jnp.arange(x.shape[1], dtype=jnp.int32)
    for l in range(DEPTH):
        lambda_init = 0.8 - 0.6 * math.exp(-0.3 * l)
        h = rmsnorm(x, g_norm1[l])
        proj = h @ w_in[l]
        y_a = mla_branch(proj, g_cq[l], w_uq[l], g_ckv[l], w_ukv[l], pos) @ w_o_mla[l]
        y_b = diff_branch(proj, lambda_qk[l], g_subln[l], lambda_init, pos) @ w_o_diff[l]
        gates = jax.nn.sigmoid((proj[..., OFF_GATE:] + b_gate[l]).astype(jnp.float32)).astype(x.dtype)
        merged = gates[..., :D_MODEL] * y_a + gates[..., D_MODEL:] * y_b
        x = x + merged @ w_out[l]
        x = x + peer_ffn(rmsnorm(x, g_norm2[l]), w_q_peer[l], sub_keys[l], expert_u[l], expert_v[l])
    return rmsnorm(x, g_final)
```

```python
import functools
import math

import jax
import jax.numpy as jnp
from jax import lax
from jax.experimental import pallas as pl
from jax.experimental.pallas import tpu as pltpu

F32 = jnp.float32
BF16 = jnp.bfloat16

EPS = 1e-6
ROPE_THETA = 10000.0
D_MODEL = 2048

MLA_HEADS = 8
MLA_Q_RANK = 768
MLA_KV_RANK = 512
MLA_NOPE = 128
MLA_ROPE = 64
MLA_V = 128
MLA_QK = MLA_NOPE + MLA_ROPE
MLA_HEAD_PAD = 256

DIFF_HEADS = 8
DIFF_HD = 64
DIFF_VD = 2 * DIFF_HD

PEER_HEADS = 8
PEER_NKEYS = 128
PEER_EXPERTS = PEER_NKEYS * PEER_NKEYS
PEER_HALF = 128
PEER_TOPK = 16

OFF_CQ = 0
OFF_CKV = OFF_CQ + MLA_Q_RANK
OFF_KR = OFF_CKV + MLA_KV_RANK
OFF_DQ = OFF_KR + MLA_ROPE
OFF_DK = OFF_DQ + DIFF_HEADS * 2 * DIFF_HD
OFF_DV = OFF_DK + DIFF_HEADS * 2 * DIFF_HD
OFF_GATE = OFF_DV + DIFF_HEADS * DIFF_VD

P_DIFF = 3 * 1024
P_MLA = 1536
P_COLS = P_DIFF + P_MLA + 2 * D_MODEL
P_TN = 512
MLA_BLK = P_DIFF // P_MLA
GATE_BLK = (P_DIFF + P_MLA) // P_TN

NEG = -1e30
VMEM_LIMIT = 56 * 1024 * 1024


def _cparams(sem):
    return pltpu.CompilerParams(dimension_semantics=sem, vmem_limit_bytes=VMEM_LIMIT)


def _rms(x, g):
    return x * lax.rsqrt(jnp.mean(x * x, axis=-1, keepdims=True) + EPS) * g


def _rope128(x, cos, sin_lo, sin_hi):
    return x * cos + pltpu.roll(x, 96, 1) * sin_lo + pltpu.roll(x, 32, 1) * sin_hi


def _proj_kernel(x_ref, g_ref, w_ref, o_ref, h_scr):
    @pl.when(pl.program_id(1) == 0)
    def _():
        h_scr[...] = _rms(x_ref[...], g_ref[...]).astype(BF16)

    o_ref[...] = jnp.dot(h_scr[...], w_ref[...], preferred_element_type=F32).astype(o_ref.dtype)


def _proj(x2d, g, w, tm):
    t, d = x2d.shape
    n = w.shape[1]
    return pl.pallas_call(
        _proj_kernel,
        grid=(t // tm, n // P_TN),
        in_specs=[pl.BlockSpec((tm, d), lambda i, j: (i, 0)),
                  pl.BlockSpec((1, d), lambda i, j: (0, 0)),
                  pl.BlockSpec((d, P_TN), lambda i, j: (0, j))],
        out_specs=pl.BlockSpec((tm, P_TN), lambda i, j: (i, j)),
        out_shape=jax.ShapeDtypeStruct((t, n), BF16),
        scratch_shapes=[pltpu.VMEM((tm, d), BF16)],
        compiler_params=_cparams(("parallel", "arbitrary")),
        name="proj",
    )(x2d, g, w)


def _diff_prep_kernel(q_ref, k_ref, cos_ref, sl_ref, sh_ref, qo_ref, ko_ref, *, scale):
    cos, sl, sh = cos_ref[...], sl_ref[...], sh_ref[...]
    for p in range(q_ref.shape[1] // 128):
        c = slice(128 * p, 128 * p + 128)
        qo_ref[:, c] = (_rope128(q_ref[:, c].astype(F32), cos, sl, sh) * scale).astype(BF16)
        ko_ref[:, c] = _rope128(k_ref[:, c].astype(F32), cos, sl, sh).astype(BF16)


def _diff_prep(p, cos, sl, sh, seq, tm):
    t = p.shape[0]
    w = DIFF_HEADS * 2 * DIFF_HD
    ns = seq // tm
    tab = pl.BlockSpec((tm, 128), lambda i: (i % ns, 0))
    return pl.pallas_call(
        functools.partial(_diff_prep_kernel, scale=DIFF_HD ** -0.5),
        grid=(t // tm,),
        in_specs=[pl.BlockSpec((tm, w), lambda i: (i, 0)),
                  pl.BlockSpec((tm, w), lambda i: (i, 1)),
                  tab, tab, tab],
        out_specs=[pl.BlockSpec((tm, w), lambda i: (i, 0)),
                   pl.BlockSpec((tm, w), lambda i: (i, 0))],
        out_shape=[jax.ShapeDtypeStruct((t, w), BF16), jax.ShapeDtypeStruct((t, w), BF16)],
        compiler_params=_cparams(("parallel",)),
        name="diff_prep",
    )(p, p, cos, sl, sh)


def _mla_prep_kernel(p_ref, gq_ref, gkv_ref, wuq_ref, wuk_ref, wuv_ref, cos_ref, sl_ref, sh_ref,
                     q_ref, k_ref, v_ref, *, scale):
    cos, sl, sh = cos_ref[...], sl_ref[...], sh_ref[...]
    cq = _rms(p_ref[:, 0:MLA_Q_RANK].astype(F32), gq_ref[...]).astype(BF16)
    yq = jnp.dot(cq, wuq_ref[...], preferred_element_type=F32)
    ckv = _rms(p_ref[:, MLA_Q_RANK:MLA_Q_RANK + MLA_KV_RANK].astype(F32), gkv_ref[...]).astype(BF16)
    kn = jnp.dot(ckv, wuk_ref[...], preferred_element_type=F32)
    v_ref[...] = jnp.dot(ckv, wuv_ref[...], preferred_element_type=F32).astype(BF16)
    kr_off = MLA_Q_RANK + MLA_KV_RANK
    kr = _rope128(p_ref[:, kr_off:kr_off + 128].astype(F32), cos, sl, sh).astype(BF16)
    for h in range(MLA_HEADS):
        a = MLA_HEAD_PAD * h
        q_ref[:, a:a + 128] = (yq[:, a:a + 128] * scale).astype(BF16)
        q_ref[:, a + 128:a + 256] = (_rope128(yq[:, a + 128:a + 256], cos, sl, sh) * scale).astype(BF16)
        k_ref[:, a:a + 128] = kn[:, 128 * h:128 * h + 128].astype(BF16)
        k_ref[:, a + 128:a + 256] = kr


def _mla_prep(p, gq, gkv, wuq, wuk, wuv, cos, sl, sh, seq, tm):
    t = p.shape[0]
    ns = seq // tm
    tab = pl.BlockSpec((tm, 128), lambda i: (i % ns, 0))
    full = lambda a: pl.BlockSpec(a.shape, lambda i: (0, 0))
    hw = MLA_HEADS * MLA_HEAD_PAD
    vw = MLA_HEADS * MLA_V
    return pl.pallas_call(
        functools.partial(_mla_prep_kernel, scale=MLA_QK ** -0.5),
        grid=(t // tm,),
        in_specs=[pl.BlockSpec((tm, P_MLA), lambda i: (i, MLA_BLK)),
                  full(gq), full(gkv), full(wuq), full(wuk), full(wuv), tab, tab, tab],
        out_specs=[pl.BlockSpec((tm, hw), lambda i: (i, 0)),
                   pl.BlockSpec((tm, hw), lambda i: (i, 0)),
                   pl.BlockSpec((tm, vw), lambda i: (i, 0))],
        out_shape=[jax.ShapeDtypeStruct((t, hw), BF16), jax.ShapeDtypeStruct((t, hw), BF16),
                   jax.ShapeDtypeStruct((t, vw), BF16)],
        compiler_params=_cparams(("parallel",)),
        name="mla_prep",
    )(p, gq, gkv, wuq, wuk, wuv, cos, sl, sh)


def _flash_step(q, k_ref, v_ref, j, m_scr, l_scr, acc_scr, tk, row0, masked):
    start = pl.multiple_of(j * tk, tk)
    k = k_ref[pl.ds(start, tk), :]
    v = v_ref[pl.ds(start, tk), :]
    s = lax.dot_general(q, k, (((1,), (1,)), ((), ())), preferred_element_type=F32)
    if masked:
        row = row0 + lax.broadcasted_iota(jnp.int32, s.shape, 0)
        col = j * tk + lax.broadcasted_iota(jnp.int32, s.shape, 1)
        s = jnp.where(col <= row, s, NEG)
    m_old = m_scr[...]
    m_new = jnp.maximum(m_old, jnp.max(s, axis=1, keepdims=True))
    alpha = jnp.exp(m_old - m_new)
    p = jnp.exp(s - m_new)
    l_scr[...] = alpha * l_scr[...] + jnp.sum(p, axis=1, keepdims=True)
    acc_scr[...] = alpha * acc_scr[...] + jnp.dot(p.astype(BF16), v, preferred_element_type=F32)
    m_scr[...] = m_new


def _flash_causal(q, k_ref, v_ref, m_scr, l_scr, acc_scr, tq, tk):
    i = pl.program_id(2)
    m_scr[...] = jnp.full(m_scr.shape, NEG, F32)
    l_scr[...] = jnp.zeros(l_scr.shape, F32)
    acc_scr[...] = jnp.zeros(acc_scr.shape, F32)
    r = tq // tk
    n_full = i * r

    def body(j, c):
        _flash_step(q, k_ref, v_ref, j, m_scr, l_scr, acc_scr, tk, i * tq, False)
        return c

    lax.fori_loop(0, n_full, body, 0)
    for d in range(r):
        _flash_step(q, k_ref, v_ref, n_full + d, m_scr, l_scr, acc_scr, tk, i * tq, True)
    return acc_scr[...] / l_scr[...]


def _mla_attn_kernel(q_ref, k_ref, v_ref, o_ref, m_scr, l_scr, acc_scr, *, tq, tk):
    o = _flash_causal(q_ref[...], k_ref, v_ref, m_scr, l_scr, acc_scr, tq, tk)
    o_ref[...] = o.astype(o_ref.dtype)


def _mla_attn(q, k, v, batch, seq, tq, tk):
    t = q.shape[0]
    nq = seq // tq
    return pl.pallas_call(
        functools.partial(_mla_attn_kernel, tq=tq, tk=tk),
        grid=(batch, MLA_HEADS, nq),
        in_specs=[pl.BlockSpec((tq, MLA_HEAD_PAD), lambda b, h, i: (b * nq + i, h)),
                  pl.BlockSpec((seq, MLA_HEAD_PAD), lambda b, h, i: (b, h)),
                  pl.BlockSpec((seq, MLA_V), lambda b, h, i: (b, h))],
        out_specs=pl.BlockSpec((tq, MLA_V), lambda b, h, i: (b * nq + i, h)),
        out_shape=jax.ShapeDtypeStruct((t, MLA_HEADS * MLA_V), BF16),
        scratch_shapes=[pltpu.VMEM((tq, 1), F32), pltpu.VMEM((tq, 1), F32),
                        pltpu.VMEM((tq, MLA_V), F32)],
        compiler_params=_cparams(("parallel", "parallel", "arbitrary")),
        name="mla_attn",
    )(q, k, v)


def _diff_attn_kernel(q_ref, k_ref, v_ref, lq_ref, g_ref, o_ref,
                      m1, l1, a1, m2, l2, a2, *, tq, tk, lambda_init):
    q = q_ref[...]
    lane = lax.broadcasted_iota(jnp.int32, q.shape, 1)
    zero = jnp.zeros_like(q)
    o1 = _flash_causal(jnp.where(lane < DIFF_HD, q, zero), k_ref, v_ref, m1, l1, a1, tq, tk)
    o2 = _flash_causal(jnp.where(lane >= DIFF_HD, q, zero), k_ref, v_ref, m2, l2, a2, tq, tk)
    lq = lq_ref[...]
    lam = (jnp.exp(jnp.sum(lq[0:1] * lq[1:2], axis=1, keepdims=True))
           - jnp.exp(jnp.sum(lq[2:3] * lq[3:4], axis=1, keepdims=True)) + lambda_init)
    o = o1 - lam * o2
    o_ref[...] = (_rms(o, g_ref[...]) * (1.0 - lambda_init)).astype(o_ref.dtype)


def _diff_attn(q, k, p, lq, g, batch, seq, tq, tk, lambda_init):
    t = q.shape[0]
    nq = seq // tq
    v_blk = (2 * DIFF_HEADS * 2 * DIFF_HD) // DIFF_VD
    return pl.pallas_call(
        functools.partial(_diff_attn_kernel, tq=tq, tk=tk, lambda_init=lambda_init),
        grid=(batch, DIFF_HEADS, nq),
        in_specs=[pl.BlockSpec((tq, 128), lambda b, h, i: (b * nq + i, h)),
                  pl.BlockSpec((seq, 128), lambda b, h, i: (b, h)),
                  pl.BlockSpec((seq, DIFF_VD), lambda b, h, i: (b, v_blk + h)),
                  pl.BlockSpec(lq.shape, lambda b, h, i: (0, 0)),
                  pl.BlockSpec(g.shape, lambda b, h, i: (0, 0))],
        out_specs=pl.BlockSpec((tq, DIFF_VD), lambda b, h, i: (b * nq + i, h)),
        out_shape=jax.ShapeDtypeStruct((t, DIFF_HEADS * DIFF_VD), BF16),
        scratch_shapes=[pltpu.VMEM((tq, 1), F32), pltpu.VMEM((tq, 1), F32), pltpu.VMEM((tq, DIFF_VD), F32),
                        pltpu.VMEM((tq, 1), F32), pltpu.VMEM((tq, 1), F32), pltpu.VMEM((tq, DIFF_VD), F32)],
        compiler_params=_cparams(("parallel", "parallel", "arbitrary")),
        name="diff_attn",
    )(q, k, p, lq, g)


def _merge_kernel(om_ref, od_ref, ga_ref, gb_ref, ba_ref, bb_ref, wom_ref, wod_ref, wout_ref,
                  x_ref, gn_ref, x2_ref, ht_ref, acc_scr):
    j = pl.program_id(1)
    ya = jnp.dot(om_ref[...], wom_ref[...], preferred_element_type=F32)
    yb = jnp.dot(od_ref[...], wod_ref[...], preferred_element_type=F32)
    ga = jax.nn.sigmoid(ga_ref[...].astype(F32) + ba_ref[...])
    gb = jax.nn.sigmoid(gb_ref[...].astype(F32) + bb_ref[...])
    merged = (ga * ya + gb * yb).astype(BF16)
    contrib = jnp.dot(merged, wout_ref[...], preferred_element_type=F32)

    @pl.when(j == 0)
    def _():
        acc_scr[...] = x_ref[...] + contrib

    @pl.when(j > 0)
    def _():
        acc_scr[...] += contrib

    @pl.when(j == pl.num_programs(1) - 1)
    def _():
        x2 = acc_scr[...]
        x2_ref[...] = x2
        ht_ref[...] = _rms(x2, gn_ref[...]).T.astype(BF16)


def _merge(om, od, p, b_gate, wom, wod, wout, x2d, gn, tm):
    t, d = x2d.shape
    tn = P_TN
    nj = d // tn
    kin = om.shape[1]
    return pl.pallas_call(
        _merge_kernel,
        grid=(t // tm, nj),
        in_specs=[pl.BlockSpec((tm, kin), lambda i, j: (i, 0)),
                  pl.BlockSpec((tm, kin), lambda i, j: (i, 0)),
                  pl.BlockSpec((tm, tn), lambda i, j: (i, GATE_BLK + j)),
                  pl.BlockSpec((tm, tn), lambda i, j: (i, GATE_BLK + nj + j)),
                  pl.BlockSpec((1, tn), lambda i, j: (0, j)),
                  pl.BlockSpec((1, tn), lambda i, j: (0, nj + j)),
                  pl.BlockSpec((kin, tn), lambda i, j: (0, j)),
                  pl.BlockSpec((kin, tn), lambda i, j: (0, j)),
                  pl.BlockSpec((tn, d), lambda i, j: (j, 0)),
                  pl.BlockSpec((tm, d), lambda i, j: (i, 0)),
                  pl.BlockSpec((1, d), lambda i, j: (0, 0))],
        out_specs=[pl.BlockSpec((tm, d), lambda i, j: (i, 0)),
                   pl.BlockSpec((d, tm), lambda i, j: (0, i))],
        out_shape=[jax.ShapeDtypeStruct((t, d), F32), jax.ShapeDtypeStruct((d, t), BF16)],
        scratch_shapes=[pltpu.VMEM((tm, d), F32)],
        compiler_params=_cparams(("parallel", "arbitrary")),
        name="merge",
    )(om, od, p, p, b_gate, b_gate, wom, wod, wout, x2d, gn)


def _top16(s, iota_r):
    rank = jnp.full(s.shape, 999.0, F32)
    vals = []
    for it in range(PEER_TOPK):
        m = jnp.max(s, axis=0, keepdims=True)
        first = jnp.min(jnp.where(s == m, iota_r, 999.0), axis=0, keepdims=True)
        sel = iota_r == first
        rank = jnp.where(sel, float(it), rank)
        s = jnp.where(sel, -jnp.inf, s)
        vals.append(m)
    return jnp.concatenate(vals, axis=0), rank


def _pair_select(v1, v2, iota_k):
    top = v1[0:1] + v2[0:1]
    cand = v1 + v2[0:1]
    cnt = jnp.zeros(v1.shape, F32)
    z = jnp.zeros(top.shape, F32)
    for _ in range(PEER_TOPK):
        m = jnp.max(cand, axis=0, keepdims=True)
        first = jnp.min(jnp.where(cand == m, iota_k, 999.0), axis=0, keepdims=True)
        sel = iota_k == first
        z = z + jnp.exp(m - top)
        cnt = jnp.where(sel, cnt + 1.0, cnt)
        taken = jnp.sum(jnp.where(sel, cnt, 0.0), axis=0, keepdims=True)
        v2n = jnp.sum(jnp.where(iota_k == taken, v2, 0.0), axis=0, keepdims=True)
        v1s = jnp.sum(jnp.where(sel, v1, 0.0), axis=0, keepdims=True)
        nxt = jnp.where(taken < float(PEER_TOPK), v1s + v2n, -jnp.inf)
        cand = jnp.where(sel, nxt, cand)
    return cnt, z


def _peer_topk_kernel(ht_ref, wqt_ref, sk_ref, cnt_ref, e1_ref, rank_ref, e2_ref, qt_scr):
    qt_scr[...] = jnp.dot(wqt_ref[...], ht_ref[...], preferred_element_type=F32)
    tn = ht_ref.shape[1]
    iota_r = lax.broadcasted_iota(jnp.int32, (PEER_NKEYS, tn), 0).astype(F32)
    iota_k = lax.broadcasted_iota(jnp.int32, (PEER_TOPK, tn), 0).astype(F32)

    def head(h, c):
        base = pl.multiple_of(h * 2 * PEER_HALF, 2 * PEER_HALF)
        q1 = qt_scr[pl.ds(base, PEER_HALF), :].astype(BF16)
        q2 = qt_scr[pl.ds(base + PEER_HALF, PEER_HALF), :].astype(BF16)
        s1 = jnp.dot(sk_ref[0], q1, preferred_element_type=F32)
        s2 = jnp.dot(sk_ref[1], q2, preferred_element_type=F32)
        v1, rank1 = _top16(s1, iota_r)
        v2, rank2 = _top16(s2, iota_r)
        cnt, z = _pair_select(v1, v2, iota_k)
        cnt_keys = jnp.zeros(s1.shape, F32)
        for i in range(PEER_TOPK):
            cnt_keys = jnp.where(rank1 == float(i), cnt[i:i + 1], cnt_keys)
        cnt_ref[h] = cnt_keys
        e1_ref[h] = jnp.exp(s1 - v1[0:1]) * (1.0 / z)
        rank_ref[h] = rank2
        e2_ref[h] = jnp.exp(s2 - v2[0:1])
        return c

    lax.fori_loop(0, PEER_HEADS, head, 0)


def _peer_topk(ht, wqt, sk, tn):
    d, t = ht.shape
    aux = jax.ShapeDtypeStruct((PEER_HEADS, PEER_NKEYS, t), F32)
    aux_spec = pl.BlockSpec((PEER_HEADS, PEER_NKEYS, tn), lambda i: (0, 0, i))
    return pl.pallas_call(
        _peer_topk_kernel,
        grid=(t // tn,),
        in_specs=[pl.BlockSpec((d, tn), lambda i: (0, i)),
                  pl.BlockSpec(wqt.shape, lambda i: (0, 0)),
                  pl.BlockSpec(sk.shape, lambda i: (0, 0, 0))],
        out_specs=[aux_spec, aux_spec, aux_spec, aux_spec],
        out_shape=[aux, aux, aux, aux],
        scratch_shapes=[pltpu.VMEM((wqt.shape[0], tn), F32)],
        compiler_params=_cparams(("parallel",)),
        name="peer_topk",
    )(ht, wqt, sk)


def _peer_dense_kernel(u_ref, vt_ref, ht_ref, cnt_ref, e1_ref, rank_ref, e2_ref, o_ref, w_scr, *, rows):
    j = pl.program_id(1)
    at = jnp.dot(u_ref[...], ht_ref[...], preferred_element_type=F32)
    for r in range(rows):
        a = at[PEER_NKEYS * r:PEER_NKEYS * (r + 1)]
        g = jnp.zeros(a.shape, F32)
        for h in range(PEER_HEADS):
            g = g + jnp.where(rank_ref[h] < cnt_ref[h, r:r + 1, :],
                              e2_ref[h] * e1_ref[h, r:r + 1, :], 0.0)
        w_scr[PEER_NKEYS * r:PEER_NKEYS * (r + 1), :] = (jax.nn.gelu(a) * g).astype(BF16)
    contrib = jnp.dot(vt_ref[...], w_scr[...], preferred_element_type=F32)

    @pl.when(j == 0)
    def _():
        o_ref[...] = contrib

    @pl.when(j > 0)
    def _():
        o_ref[...] += contrib


def _peer_dense(u, vt, ht, cnt, e1, rank, e2, tn, rows):
    e, d = u.shape
    t = ht.shape[1]
    te = rows * PEER_NKEYS
    row_spec = pl.BlockSpec((PEER_HEADS, rows, tn), lambda i, j: (0, j, i))
    col_spec = pl.BlockSpec((PEER_HEADS, PEER_NKEYS, tn), lambda i, j: (0, 0, i))
    return pl.pallas_call(
        functools.partial(_peer_dense_kernel, rows=rows),
        grid=(t // tn, e // te),
        in_specs=[pl.BlockSpec((te, d), lambda i, j: (j, 0)),
                  pl.BlockSpec((d, te), lambda i, j: (0, j)),
                  pl.BlockSpec((d, tn), lambda i, j: (0, i)),
                  row_spec, row_spec, col_spec, col_spec],
        out_specs=pl.BlockSpec((d, tn), lambda i, j: (0, i)),
        out_shape=jax.ShapeDtypeStruct((d, t), F32),
        scratch_shapes=[pltpu.VMEM((te, tn), BF16)],
        compiler_params=_cparams(("parallel", "arbitrary")),
        name="peer_dense",
    )(u, vt, ht, cnt, e1, rank, e2)


def _final_kernel(x_ref, pt_ref, g_ref, o_ref):
    o_ref[...] = _rms(x_ref[...] + pt_ref[...].T, g_ref[...])


def _final(x2, pt, g, tm):
    t, d = x2.shape
    return pl.pallas_call(
        _final_kernel,
        grid=(t // tm,),
        in_specs=[pl.BlockSpec((tm, d), lambda i: (i, 0)),
                  pl.BlockSpec((d, tm), lambda i: (0, i)),
                  pl.BlockSpec((1, d), lambda i: (0, 0))],
        out_specs=pl.BlockSpec((tm, d), lambda i: (i, 0)),
        out_shape=jax.ShapeDtypeStruct((t, d), F32),
        compiler_params=_cparams(("parallel",)),
        name="final",
    )(x2, pt, g)


def _rope_tables(seq):
    half = MLA_ROPE // 2
    inv = ROPE_THETA ** (-jnp.arange(0, MLA_ROPE, 2, dtype=F32) / MLA_ROPE)
    ang = jnp.arange(seq, dtype=F32)[:, None] * inv[None, :]
    cos = jnp.tile(jnp.cos(ang), (1, 128 // half))
    sin = jnp.tile(jnp.sin(ang), (1, 128 // half))
    first_half = (jnp.arange(128) % MLA_ROPE) < half
    return cos, jnp.where(first_half, -sin, 0.0), jnp.where(first_half, 0.0, sin)


def _tiles(seq):
    pick = lambda pref: max(c for c in (128, 256, 512, 1024) if c <= pref and seq % c == 0)
    return dict(proj=pick(1024), prep=pick(512), attn=pick(512), merge=pick(512),
                topk=pick(256), dense=pick(512), final=pick(512))


def kernel(x, w_in, b_gate, g_norm1, g_cq, w_uq, g_ckv, w_ukv, w_o_mla, lambda_qk, g_subln,
           w_o_diff, w_out, g_norm2, w_q_peer, sub_keys, expert_u, expert_v, g_final):
    batch, seq, d = x.shape
    assert d == D_MODEL and seq % 128 == 0
    assert w_in.shape[0] == 1, "single-layer trunk"
    l = 0
    lambda_init = 0.8 - 0.6 * math.exp(-0.3 * l)
    tl = _tiles(seq)
    cos, sl, sh = _rope_tables(seq)
    x2d = x.reshape(batch * seq, d)
    row = lambda v: v.reshape(1, -1)

    wl = w_in[l]
    zpad = jnp.zeros((d, P_MLA - (MLA_Q_RANK + MLA_KV_RANK + MLA_ROPE)), F32)
    w_p = jnp.concatenate([wl[:, OFF_DQ:OFF_GATE], wl[:, OFF_CQ:OFF_DQ], zpad, wl[:, OFF_GATE:]],
                          axis=1).astype(BF16)
    wuq = w_uq[l].reshape(MLA_Q_RANK, MLA_HEADS, MLA_QK)
    wuq = jnp.pad(wuq, ((0, 0), (0, 0), (0, MLA_HEAD_PAD - MLA_QK)))
    wuq = wuq.reshape(MLA_Q_RANK, MLA_HEADS * MLA_HEAD_PAD).astype(BF16)
    wukv = w_ukv[l].reshape(MLA_KV_RANK, MLA_HEADS, MLA_NOPE + MLA_V)
    wuk = wukv[:, :, :MLA_NOPE].reshape(MLA_KV_RANK, MLA_HEADS * MLA_NOPE).astype(BF16)
    wuv = wukv[:, :, MLA_NOPE:].reshape(MLA_KV_RANK, MLA_HEADS * MLA_V).astype(BF16)

    p = _proj(x2d, row(g_norm1[l]), w_p, tl["proj"])
    qd, kd = _diff_prep(p, cos, sl, sh, seq, tl["prep"])
    qm, km, vm = _mla_prep(p, row(g_cq[l]), row(g_ckv[l]), wuq, wuk, wuv, cos, sl, sh, seq, tl["prep"])
    om = _mla_attn(qm, km, vm, batch, seq, tl["attn"], tl["attn"])
    od = _diff_attn(qd, kd, p, lambda_qk[l], row(g_subln[l]), batch, seq, tl["attn"], tl["attn"],
                    lambda_init)
    x2, ht = _merge(om, od, p, row(b_gate[l]), w_o_mla[l].astype(BF16), w_o_diff[l].astype(BF16),
                    w_out[l].astype(BF16), x2d, row(g_norm2[l]), tl["merge"])
    cnt, e1, rank, e2 = _peer_topk(ht, w_q_peer[l].T.astype(BF16), sub_keys[l].astype(BF16), tl["topk"])
    pt = _peer_dense(expert_u[l].astype(BF16), expert_v[l].T.astype(BF16), ht, cnt, e1, rank, e2,
                     tl["dense"], 8)
    out = _final(x2, pt, row(g_final), tl["final"])
    return out.reshape(batch, seq, d)
```

```python
import functools
import math

import jax
import jax.numpy as jnp
from jax import lax
from jax.experimental import pallas as pl
from jax.experimental.pallas import tpu as pltpu

F32 = jnp.float32
BF16 = jnp.bfloat16

EPS = 1e-6
ROPE_THETA = 10000.0
D_MODEL = 2048
LOG2E = 1.4426950408889634

MLA_HEADS = 8
MLA_Q_RANK = 768
MLA_KV_RANK = 512
MLA_NOPE = 128
MLA_ROPE = 64
MLA_V = 128
MLA_QK = MLA_NOPE + MLA_ROPE
MLA_HEAD_PAD = 256

DIFF_HEADS = 8
DIFF_HD = 64
DIFF_VD = 2 * DIFF_HD

PEER_HEADS = 8
PEER_NKEYS = 128
PEER_EXPERTS = PEER_NKEYS * PEER_NKEYS
PEER_HALF = 128
PEER_TOPK = 16

OFF_CQ = 0
OFF_CKV = OFF_CQ + MLA_Q_RANK
OFF_KR = OFF_CKV + MLA_KV_RANK
OFF_DQ = OFF_KR + MLA_ROPE
OFF_DK = OFF_DQ + DIFF_HEADS * 2 * DIFF_HD
OFF_DV = OFF_DK + DIFF_HEADS * 2 * DIFF_HD
OFF_GATE = OFF_DV + DIFF_HEADS * DIFF_VD

P_DIFF = 3 * 1024
P_MLA = 1536
P_COLS = P_DIFF + P_MLA + 2 * D_MODEL
P_TN = 512
MLA_BLK = P_DIFF // P_MLA
GATE_BLK = (P_DIFF + P_MLA) // P_TN

NEG = -1e30
VMEM_LIMIT = 56 * 1024 * 1024


def _cparams(sem):
    return pltpu.CompilerParams(dimension_semantics=sem, vmem_limit_bytes=VMEM_LIMIT)


def _rms(x, g):
    return x * lax.rsqrt(jnp.mean(x * x, axis=-1, keepdims=True) + EPS) * g


def _rope128(x, cos, sin_lo, sin_hi):
    return x * cos + pltpu.roll(x, 96, 1) * sin_lo + pltpu.roll(x, 32, 1) * sin_hi


def _proj_kernel(x_ref, g_ref, w_ref, o_ref, h_scr):
    @pl.when(pl.program_id(1) == 0)
    def _():
        h_scr[...] = _rms(x_ref[...], g_ref[...]).astype(BF16)

    o_ref[...] = jnp.dot(h_scr[...], w_ref[...], preferred_element_type=F32).astype(o_ref.dtype)


def _proj(x2d, g, w, tm):
    t, d = x2d.shape
    n = w.shape[1]
    return pl.pallas_call(
        _proj_kernel,
        grid=(t // tm, n // P_TN),
        in_specs=[pl.BlockSpec((tm, d), lambda i, j: (i, 0)),
                  pl.BlockSpec((1, d), lambda i, j: (0, 0)),
                  pl.BlockSpec((d, P_TN), lambda i, j: (0, j))],
        out_specs=pl.BlockSpec((tm, P_TN), lambda i, j: (i, j)),
        out_shape=jax.ShapeDtypeStruct((t, n), BF16),
        scratch_shapes=[pltpu.VMEM((tm, d), BF16)],
        compiler_params=_cparams(("parallel", "arbitrary")),
        name="proj",
    )(x2d, g, w)


def _diff_prep_kernel(q_ref, k_ref, v_ref, cos_ref, sl_ref, sh_ref, qt_ref, ko_ref, vt_ref, *, scale):
    cos, sl, sh = cos_ref[...], sl_ref[...], sh_ref[...]
    for h in range(DIFF_HEADS):
        c = slice(128 * h, 128 * h + 128)
        q = _rope128(q_ref[:, c].astype(F32), cos, sl, sh) * scale
        qt_ref[h] = q.T.astype(BF16)
        ko_ref[:, c] = _rope128(k_ref[:, c].astype(F32), cos, sl, sh).astype(BF16)
        vt_ref[h] = v_ref[:, c].astype(F32).T.astype(BF16)


def _diff_prep(p, cos, sl, sh, batch, seq, ta):
    t = p.shape[0]
    w = DIFF_HEADS * 2 * DIFF_HD
    ns = seq // ta
    tab = pl.BlockSpec((ta, 128), lambda i: (i % ns, 0))
    tr_shape = jax.ShapeDtypeStruct((batch, DIFF_HEADS, ns, 128, ta), BF16)
    tr_spec = pl.BlockSpec((None, DIFF_HEADS, None, 128, ta), lambda i: (i // ns, 0, i % ns, 0, 0))
    return pl.pallas_call(
        functools.partial(_diff_prep_kernel, scale=DIFF_HD ** -0.5 * LOG2E),
        grid=(t // ta,),
        in_specs=[pl.BlockSpec((ta, w), lambda i: (i, 0)),
                  pl.BlockSpec((ta, w), lambda i: (i, 1)),
                  pl.BlockSpec((ta, w), lambda i: (i, 2)),
                  tab, tab, tab],
        out_specs=[tr_spec, pl.BlockSpec((ta, w), lambda i: (i, 0)), tr_spec],
        out_shape=[tr_shape, jax.ShapeDtypeStruct((t, w), BF16), tr_shape],
        compiler_params=_cparams(("parallel",)),
        name="diff_prep",
    )(p, p, p, cos, sl, sh)


def _mla_prep_kernel(p_ref, gq_ref, gkv_ref, wuq_ref, wuk_ref, wuv_ref, cos_ref, sl_ref, sh_ref,
                     qt_ref, k_ref, vt_ref, *, scale):
    cos, sl, sh = cos_ref[...], sl_ref[...], sh_ref[...]
    cq = _rms(p_ref[:, 0:MLA_Q_RANK].astype(F32), gq_ref[...]).astype(BF16)
    yq = jnp.dot(cq, wuq_ref[...], preferred_element_type=F32)
    ckv = _rms(p_ref[:, MLA_Q_RANK:MLA_Q_RANK + MLA_KV_RANK].astype(F32), gkv_ref[...]).astype(BF16)
    kn = jnp.dot(ckv, wuk_ref[...], preferred_element_type=F32)
    vv = jnp.dot(ckv, wuv_ref[...], preferred_element_type=F32)
    kr_off = MLA_Q_RANK + MLA_KV_RANK
    kr = _rope128(p_ref[:, kr_off:kr_off + 128].astype(F32), cos, sl, sh).astype(BF16)
    for h in range(MLA_HEADS):
        a = MLA_HEAD_PAD * h
        qt_ref[h, 0:128, :] = (yq[:, a:a + 128] * scale).T.astype(BF16)
        qt_ref[h, 128:256, :] = (_rope128(yq[:, a + 128:a + 256], cos, sl, sh) * scale).T.astype(BF16)
        k_ref[:, a:a + 128] = kn[:, 128 * h:128 * h + 128].astype(BF16)
        k_ref[:, a + 128:a + 256] = kr
        vt_ref[h] = vv[:, 128 * h:128 * h + 128].T.astype(BF16)


def _mla_prep(p, gq, gkv, wuq, wuk, wuv, cos, sl, sh, batch, seq, ta):
    t = p.shape[0]
    ns = seq // ta
    tab = pl.BlockSpec((ta, 128), lambda i: (i % ns, 0))
    full = lambda a: pl.BlockSpec(a.shape, lambda i: (0, 0))
    hw = MLA_HEADS * MLA_HEAD_PAD
    tr = lambda rows: (jax.ShapeDtypeStruct((batch, MLA_HEADS, ns, rows, ta), BF16),
                       pl.BlockSpec((None, MLA_HEADS, None, rows, ta), lambda i: (i // ns, 0, i % ns, 0, 0)))
    qt_shape, qt_spec = tr(MLA_HEAD_PAD)
    vt_shape, vt_spec = tr(MLA_V)
    return pl.pallas_call(
        functools.partial(_mla_prep_kernel, scale=MLA_QK ** -0.5 * LOG2E),
        grid=(t // ta,),
        in_specs=[pl.BlockSpec((ta, P_MLA), lambda i: (i, MLA_BLK)),
                  full(gq), full(gkv), full(wuq), full(wuk), full(wuv), tab, tab, tab],
        out_specs=[qt_spec, pl.BlockSpec((ta, hw), lambda i: (i, 0)), vt_spec],
        out_shape=[qt_shape, jax.ShapeDtypeStruct((t, hw), BF16), vt_shape],
        compiler_params=_cparams(("parallel",)),
        name="mla_prep",
    )(p, gq, gkv, wuq, wuk, wuv, cos, sl, sh)


def _attn_maps(qts, k_ref, vt_ref, bias_ref, s_a, s_b, m_scrs, l_scrs, acc_scrs, ta):
    n = pl.program_id(2)
    for m_scr, l_scr, acc_scr in zip(m_scrs, l_scrs, acc_scrs):
        m_scr[...] = jnp.full(m_scr.shape, NEG, F32)
        l_scr[...] = jnp.zeros(l_scr.shape, F32)
        acc_scr[...] = jnp.zeros(acc_scr.shape, F32)

    def scores(bufs, j):
        start = pl.multiple_of(jnp.minimum(j, n) * ta, ta)
        kj = k_ref[pl.ds(start, ta), :]
        for buf, qt in zip(bufs, qts):
            buf[...] = jnp.dot(kj, qt, preferred_element_type=F32)

    def reduce(bufs, j):
        bias = bias_ref[jnp.where(j < n, 0, jnp.where(j == n, 1, 2))]
        vt = vt_ref[jnp.minimum(j, n)]
        for buf, m_scr, l_scr, acc_scr in zip(bufs, m_scrs, l_scrs, acc_scrs):
            s = buf[...] + bias
            m_old = m_scr[...]
            m_new = jnp.maximum(m_old, jnp.max(s, axis=0, keepdims=True))
            alpha = jnp.exp2(m_old - m_new)
            p = jnp.exp2(s - m_new)
            l_scr[...] = alpha * l_scr[...] + jnp.sum(p, axis=0, keepdims=True)
            acc_scr[...] = alpha * acc_scr[...] + jnp.dot(vt, p.astype(BF16), preferred_element_type=F32)
            m_scr[...] = m_new

    scores(s_a, 0)

    def pair(pp, c):
        j0 = 2 * pp
        scores(s_b, j0 + 1)
        reduce(s_a, j0)
        scores(s_a, j0 + 2)
        reduce(s_b, j0 + 1)
        return c

    lax.fori_loop(0, (n + 2) // 2, pair, 0)
    return [acc_scr[...] / l_scr[...] for l_scr, acc_scr in zip(l_scrs, acc_scrs)]


def _mla_attn_kernel(qt_ref, k_ref, vt_ref, bias_ref, o_ref, sa, sb, m_scr, l_scr, acc_scr, *, ta):
    (o,) = _attn_maps([qt_ref[...]], k_ref, vt_ref, bias_ref, [sa], [sb], [m_scr], [l_scr], [acc_scr], ta)
    o_ref[...] = o.T.astype(o_ref.dtype)


def _attn_scratch(nmaps, dv, ta):
    return ([pltpu.VMEM((ta, ta), F32)] * (2 * nmaps)
            + [pltpu.VMEM((1, ta), F32), pltpu.VMEM((1, ta), F32), pltpu.VMEM((dv, ta), F32)] * nmaps)


def _mla_attn(qt, k, vt, bias, batch, seq, ta):
    t = k.shape[0]
    nq = seq // ta
    return pl.pallas_call(
        functools.partial(_mla_attn_kernel, ta=ta),
        grid=(batch, MLA_HEADS, nq),
        in_specs=[pl.BlockSpec((None, None, None, MLA_HEAD_PAD, ta), lambda b, h, i: (b, h, i, 0, 0)),
                  pl.BlockSpec((seq, MLA_HEAD_PAD), lambda b, h, i: (b, h)),
                  pl.BlockSpec((None, None, nq, MLA_V, ta), lambda b, h, i: (b, h, 0, 0, 0)),
                  pl.BlockSpec(bias.shape, lambda b, h, i: (0, 0, 0))],
        out_specs=pl.BlockSpec((ta, MLA_V), lambda b, h, i: (b * nq + i, h)),
        out_shape=jax.ShapeDtypeStruct((t, MLA_HEADS * MLA_V), BF16),
        scratch_shapes=_attn_scratch(1, MLA_V, ta),
        compiler_params=_cparams(("parallel", "parallel", "arbitrary")),
        name="mla_attn",
    )(qt, k, vt, bias)


def _diff_attn_kernel(qt_ref, k_ref, vt_ref, bias_ref, lq_ref, g_ref, o_ref,
                      sa1, sa2, sb1, sb2, m1, l1, a1, m2, l2, a2, *, ta, lambda_init):
    qt = qt_ref[...]
    row = lax.broadcasted_iota(jnp.int32, qt.shape, 0)
    zero = jnp.zeros_like(qt)
    qts = [jnp.where(row < DIFF_HD, qt, zero), jnp.where(row >= DIFF_HD, qt, zero)]
    o1, o2 = _attn_maps(qts, k_ref, vt_ref, bias_ref, [sa1, sa2], [sb1, sb2],
                        [m1, m2], [l1, l2], [a1, a2], ta)
    lq = lq_ref[...]
    lam = (jnp.exp(jnp.sum(lq[0:1] * lq[1:2], axis=1, keepdims=True))
           - jnp.exp(jnp.sum(lq[2:3] * lq[3:4], axis=1, keepdims=True)) + lambda_init)
    o = (o1 - lam * o2).T
    o_ref[...] = (_rms(o, g_ref[...]) * (1.0 - lambda_init)).astype(o_ref.dtype)


def _diff_attn(qt, k, vt, bias, lq, g, batch, seq, ta, lambda_init):
    t = k.shape[0]
    nq = seq // ta
    return pl.pallas_call(
        functools.partial(_diff_attn_kernel, ta=ta, lambda_init=lambda_init),
        grid=(batch, DIFF_HEADS, nq),
        in_specs=[pl.BlockSpec((None, None, None, 128, ta), lambda b, h, i: (b, h, i, 0, 0)),
                  pl.BlockSpec((seq, 128), lambda b, h, i: (b, h)),
                  pl.BlockSpec((None, None, nq, DIFF_VD, ta), lambda b, h, i: (b, h, 0, 0, 0)),
                  pl.BlockSpec(bias.shape, lambda b, h, i: (0, 0, 0)),
                  pl.BlockSpec(lq.shape, lambda b, h, i: (0, 0)),
                  pl.BlockSpec(g.shape, lambda b, h, i: (0, 0))],
        out_specs=pl.BlockSpec((ta, DIFF_VD), lambda b, h, i: (b * nq + i, h)),
        out_shape=jax.ShapeDtypeStruct((t, DIFF_HEADS * DIFF_VD), BF16),
        scratch_shapes=_attn_scratch(2, DIFF_VD, ta),
        compiler_params=_cparams(("parallel", "parallel", "arbitrary")),
        name="diff_attn",
    )(qt, k, vt, bias, lq, g)


def _merge_kernel(om_ref, od_ref, ga_ref, gb_ref, ba_ref, bb_ref, wom_ref, wod_ref, wout_ref,
                  x_ref, gn_ref, x2_ref, ht_ref, acc_scr):
    j = pl.program_id(1)
    ya = jnp.dot(om_ref[...], wom_ref[...], preferred_element_type=F32)
    yb = jnp.dot(od_ref[...], wod_ref[...], preferred_element_type=F32)
    ga = jax.nn.sigmoid(ga_ref[...].astype(F32) + ba_ref[...])
    gb = jax.nn.sigmoid(gb_ref[...].astype(F32) + bb_ref[...])
    merged = (ga * ya + gb * yb).astype(BF16)
    contrib = jnp.dot(merged, wout_ref[...], preferred_element_type=F32)

    @pl.when(j == 0)
    def _():
        acc_scr[...] = x_ref[...] + contrib

    @pl.when(j > 0)
    def _():
        acc_scr[...] += contrib

    @pl.when(j == pl.num_programs(1) - 1)
    def _():
        x2 = acc_scr[...]
        x2_ref[...] = x2
        ht_ref[...] = _rms(x2, gn_ref[...]).T.astype(BF16)


def _merge(om, od, p, b_gate, wom, wod, wout, x2d, gn, tm):
    t, d = x2d.shape
    tn = P_TN
    nj = d // tn
    kin = om.shape[1]
    return pl.pallas_call(
        _merge_kernel,
        grid=(t // tm, nj),
        in_specs=[pl.BlockSpec((tm, kin), lambda i, j: (i, 0)),
                  pl.BlockSpec((tm, kin), lambda i, j: (i, 0)),
                  pl.BlockSpec((tm, tn), lambda i, j: (i, GATE_BLK + j)),
                  pl.BlockSpec((tm, tn), lambda i, j: (i, GATE_BLK + nj + j)),
                  pl.BlockSpec((1, tn), lambda i, j: (0, j)),
                  pl.BlockSpec((1, tn), lambda i, j: (0, nj + j)),
                  pl.BlockSpec((kin, tn), lambda i, j: (0, j)),
                  pl.BlockSpec((kin, tn), lambda i, j: (0, j)),
                  pl.BlockSpec((tn, d), lambda i, j: (j, 0)),
                  pl.BlockSpec((tm, d), lambda i, j: (i, 0)),
                  pl.BlockSpec((1, d), lambda i, j: (0, 0))],
        out_specs=[pl.BlockSpec((tm, d), lambda i, j: (i, 0)),
                   pl.BlockSpec((d, tm), lambda i, j: (0, i))],
        out_shape=[jax.ShapeDtypeStruct((t, d), F32), jax.ShapeDtypeStruct((d, t), BF16)],
        scratch_shapes=[pltpu.VMEM((tm, d), F32)],
        compiler_params=_cparams(("parallel", "arbitrary")),
        name="merge",
    )(om, od, p, p, b_gate, b_gate, wom, wod, wout, x2d, gn)


def _top16(s, iota_r):
    rank = jnp.full(s.shape, 999.0, F32)
    vals = []
    for it in range(PEER_TOPK):
        m = jnp.max(s, axis=0, keepdims=True)
        first = jnp.min(jnp.where(s == m, iota_r, 999.0), axis=0, keepdims=True)
        sel = iota_r == first
        rank = jnp.where(sel, float(it), rank)
        s = jnp.where(sel, -jnp.inf, s)
        vals.append(m)
    return jnp.concatenate(vals, axis=0), rank


def _pair_select(v1, v2, iota_k):
    top = v1[0:1] + v2[0:1]
    cand = v1 + v2[0:1]
    cnt = jnp.zeros(v1.shape, F32)
    z = jnp.zeros(top.shape, F32)
    for _ in range(PEER_TOPK):
        m = jnp.max(cand, axis=0, keepdims=True)
        first = jnp.min(jnp.where(cand == m, iota_k, 999.0), axis=0, keepdims=True)
        sel = iota_k == first
        z = z + jnp.exp(m - top)
        cnt = jnp.where(sel, cnt + 1.0, cnt)
        taken = jnp.sum(jnp.where(sel, cnt, 0.0), axis=0, keepdims=True)
        v2n = jnp.sum(jnp.where(iota_k == taken, v2, 0.0), axis=0, keepdims=True)
        v1s = jnp.sum(jnp.where(sel, v1, 0.0), axis=0, keepdims=True)
        nxt = jnp.where(taken < float(PEER_TOPK), v1s + v2n, -jnp.inf)
        cand = jnp.where(sel, nxt, cand)
    return cnt, z


def _peer_topk_kernel(ht_ref, wqt_ref, sk_ref, cnt_ref, e1_ref, rank_ref, e2_ref, qt_scr):
    qt_scr[...] = jnp.dot(wqt_ref[...], ht_ref[...], preferred_element_type=F32)
    tn = ht_ref.shape[1]
    iota_r = lax.broadcasted_iota(jnp.int32, (PEER_NKEYS, tn), 0).astype(F32)
    iota_k = lax.broadcasted_iota(jnp.int32, (PEER_TOPK, tn), 0).astype(F32)

    def head(h, c):
        base = pl.multiple_of(h * 2 * PEER_HALF, 2 * PEER_HALF)
        q1 = qt_scr[pl.ds(base, PEER_HALF), :].astype(BF16)
        q2 = qt_scr[pl.ds(base + PEER_HALF, PEER_HALF), :].astype(BF16)
        s1 = jnp.dot(sk_ref[0], q1, preferred_element_type=F32)
        s2 = jnp.dot(sk_ref[1], q2, preferred_element_type=F32)
        v1, rank1 = _top16(s1, iota_r)
        v2, rank2 = _top16(s2, iota_r)
        cnt, z = _pair_select(v1, v2, iota_k)
        cnt_keys = jnp.zeros(s1.shape, F32)
        for i in range(PEER_TOPK):
            cnt_keys = jnp.where(rank1 == float(i), cnt[i:i + 1], cnt_keys)
        cnt_ref[h] = cnt_keys
        e1_ref[h] = jnp.exp(s1 - v1[0:1]) * (1.0 / z)
        rank_ref[h] = rank2
        e2_ref[h] = jnp.exp(s2 - v2[0:1])
        return c

    lax.fori_loop(0, PEER_HEADS, head, 0)


def _peer_topk(ht, wqt, sk, tn):
    d, t = ht.shape
    aux = jax.ShapeDtypeStruct((PEER_HEADS, PEER_NKEYS, t), F32)
    aux_spec = pl.BlockSpec((PEER_HEADS, PEER_NKEYS, tn), lambda i: (0, 0, i))
    return pl.pallas_call(
        _peer_topk_kernel,
        grid=(t // tn,),
        in_specs=[pl.BlockSpec((d, tn), lambda i: (0, i)),
                  pl.BlockSpec(wqt.shape, lambda i: (0, 0)),
                  pl.BlockSpec(sk.shape, lambda i: (0, 0, 0))],
        out_specs=[aux_spec, aux_spec, aux_spec, aux_spec],
        out_shape=[aux, aux, aux, aux],
        scratch_shapes=[pltpu.VMEM((wqt.shape[0], tn), F32)],
        compiler_params=_cparams(("parallel",)),
        name="peer_topk",
    )(ht, wqt, sk)


def _peer_dense_kernel(u_ref, vt_ref, ht_ref, cnt_ref, e1_ref, rank_ref, e2_ref, o_ref, w_scr, *, rows):
    j = pl.program_id(1)
    at = jnp.dot(u_ref[...], ht_ref[...], preferred_element_type=F32)
    for r in range(rows):
        a = at[PEER_NKEYS * r:PEER_NKEYS * (r + 1)]
        g = jnp.zeros(a.shape, F32)
        for h in range(PEER_HEADS):
            g = g + jnp.where(rank_ref[h] < cnt_ref[h, r:r + 1, :],
                              e2_ref[h] * e1_ref[h, r:r + 1, :], 0.0)
        w_scr[PEER_NKEYS * r:PEER_NKEYS * (r + 1), :] = (jax.nn.gelu(a) * g).astype(BF16)
    contrib = jnp.dot(vt_ref[...], w_scr[...], preferred_element_type=F32)

    @pl.when(j == 0)
    def _():
        o_ref[...] = contrib

    @pl.when(j > 0)
    def _():
        o_ref[...] += contrib


def _peer_dense(u, vt, ht, cnt, e1, rank, e2, tn, rows):
    e, d = u.shape
    t = ht.shape[1]
    te = rows * PEER_NKEYS
    row_spec = pl.BlockSpec((PEER_HEADS, rows, tn), lambda i, j: (0, j, i))
    col_spec = pl.BlockSpec((PEER_HEADS, PEER_NKEYS, tn), lambda i, j: (0, 0, i))
    return pl.pallas_call(
        functools.partial(_peer_dense_kernel, rows=rows),
        grid=(t // tn, e // te),
        in_specs=[pl.BlockSpec((te, d), lambda i, j: (j, 0)),
                  pl.BlockSpec((d, te), lambda i, j: (0, j)),
                  pl.BlockSpec((d, tn), lambda i, j: (0, i)),
                  row_spec, row_spec, col_spec, col_spec],
        out_specs=pl.BlockSpec((d, tn), lambda i, j: (0, i)),
        out_shape=jax.ShapeDtypeStruct((d, t), F32),
        scratch_shapes=[pltpu.VMEM((te, tn), BF16)],
        compiler_params=_cparams(("parallel", "arbitrary")),
        name="peer_dense",
    )(u, vt, ht, cnt, e1, rank, e2)


def _final_kernel(x_ref, pt_ref, g_ref, o_ref):
    o_ref[...] = _rms(x_ref[...] + pt_ref[...].T, g_ref[...])


def _final(x2, pt, g, tm):
    t, d = x2.shape
    return pl.pallas_call(
        _final_kernel,
        grid=(t // tm,),
        in_specs=[pl.BlockSpec((tm, d), lambda i: (i, 0)),
                  pl.BlockSpec((d, tm), lambda i: (0, i)),
                  pl.BlockSpec((1, d), lambda i: (0, 0))],
        out_specs=pl.BlockSpec((tm, d), lambda i: (i, 0)),
        out_shape=jax.ShapeDtypeStruct((t, d), F32),
        compiler_params=_cparams(("parallel",)),
        name="final",
    )(x2, pt, g)


def _rope_tables(seq):
    half = MLA_ROPE // 2
    inv = ROPE_THETA ** (-jnp.arange(0, MLA_ROPE, 2, dtype=F32) / MLA_ROPE)
    ang = jnp.arange(seq, dtype=F32)[:, None] * inv[None, :]
    cos = jnp.tile(jnp.cos(ang), (1, 128 // half))
    sin = jnp.tile(jnp.sin(ang), (1, 128 // half))
    first_half = (jnp.arange(128) % MLA_ROPE) < half
    return cos, jnp.where(first_half, -sin, 0.0), jnp.where(first_half, 0.0, sin)


def _attn_bias(ta):
    key = jnp.arange(ta)[:, None]
    qry = jnp.arange(ta)[None, :]
    diag = jnp.where(key <= qry, 0.0, NEG).astype(F32)
    return jnp.stack([jnp.zeros((ta, ta), F32), diag, jnp.full((ta, ta), NEG, F32)])


def _tiles(seq):
    pick = lambda pref: max(c for c in (128, 256, 512, 1024) if c <= pref and seq % c == 0)
    return dict(proj=pick(1024), attn=pick(512), merge=pick(512),
                topk=pick(256), dense=pick(512), final=pick(512))


def kernel(x, w_in, b_gate, g_norm1, g_cq, w_uq, g_ckv, w_ukv, w_o_mla, lambda_qk, g_subln,
           w_o_diff, w_out, g_norm2, w_q_peer, sub_keys, expert_u, expert_v, g_final):
    batch, seq, d = x.shape
    assert d == D_MODEL and seq % 128 == 0
    assert w_in.shape[0] == 1, "single-layer trunk"
    l = 0
    lambda_init = 0.8 - 0.6 * math.exp(-0.3 * l)
    tl = _tiles(seq)
    ta = tl["attn"]
    cos, sl, sh = _rope_tables(seq)
    bias = _attn_bias(ta)
    x2d = x.reshape(batch * seq, d)
    row = lambda v: v.reshape(1, -1)

    wl = w_in[l]
    zpad = jnp.zeros((d, P_MLA - (MLA_Q_RANK + MLA_KV_RANK + MLA_ROPE)), F32)
    w_p = jnp.concatenate([wl[:, OFF_DQ:OFF_GATE], wl[:, OFF_CQ:OFF_DQ], zpad, wl[:, OFF_GATE:]],
                          axis=1).astype(BF16)
    wuq = w_uq[l].reshape(MLA_Q_RANK, MLA_HEADS, MLA_QK)
    wuq = jnp.pad(wuq, ((0, 0), (0, 0), (0, MLA_HEAD_PAD - MLA_QK)))
    wuq = wuq.reshape(MLA_Q_RANK, MLA_HEADS * MLA_HEAD_PAD).astype(BF16)
    wukv = w_ukv[l].reshape(MLA_KV_RANK, MLA_HEADS, MLA_NOPE + MLA_V)
    wuk = wukv[:, :, :MLA_NOPE].reshape(MLA_KV_RANK, MLA_HEADS * MLA_NOPE).astype(BF16)
    wuv = wukv[:, :, MLA_NOPE:].reshape(MLA_KV_RANK, MLA_HEADS * MLA_V).astype(BF16)

    p = _proj(x2d, row(g_norm1[l]), w_p, tl["proj"])
    qdt, kd, vdt = _diff_prep(p, cos, sl, sh, batch, seq, ta)
    qmt, km, vmt = _mla_prep(p, row(g_cq[l]), row(g_ckv[l]), wuq, wuk, wuv, cos, sl, sh, batch, seq, ta)
    om = _mla_attn(qmt, km, vmt, bias, batch, seq, ta)
    od = _diff_attn(qdt, kd, vdt, bias, lambda_qk[l], row(g_subln[l]), batch, seq, ta, lambda_init)
    x2, ht = _merge(om, od, p, row(b_gate[l]), w_o_mla[l].astype(BF16), w_o_diff[l].astype(BF16),
                    w_out[l].astype(BF16), x2d, row(g_norm2[l]), tl["merge"])
    cnt, e1, rank, e2 = _peer_topk(ht, w_q_peer[l].T.astype(BF16), sub_keys[l].astype(BF16), tl["topk"])
    pt = _peer_dense(expert_u[l].astype(BF16), expert_v[l].T.astype(BF16), ht, cnt, e1, rank, e2,
                     tl["dense"], 8)
    out = _final(x2, pt, row(g_final), tl["final"])
    return out.reshape(batch, seq, d)
```

```python
import functools
import math

import jax
import jax.numpy as jnp
from jax import lax
from jax.experimental import pallas as pl
from jax.experimental.pallas import tpu as pltpu

F32 = jnp.float32
BF16 = jnp.bfloat16

EPS = 1e-6
ROPE_THETA = 10000.0
D_MODEL = 2048
LOG2E = 1.4426950408889634

MLA_HEADS = 8
MLA_Q_RANK = 768
MLA_KV_RANK = 512
MLA_NOPE = 128
MLA_ROPE = 64
MLA_V = 128
MLA_QK = MLA_NOPE + MLA_ROPE
MLA_HEAD_PAD = 256

DIFF_HEADS = 8
DIFF_HD = 64
DIFF_VD = 2 * DIFF_HD

PEER_HEADS = 8
PEER_NKEYS = 128
PEER_EXPERTS = PEER_NKEYS * PEER_NKEYS
PEER_HALF = 128
PEER_TOPK = 16
RANK_OUT = 64.0

OFF_CQ = 0
OFF_CKV = OFF_CQ + MLA_Q_RANK
OFF_KR = OFF_CKV + MLA_KV_RANK
OFF_DQ = OFF_KR + MLA_ROPE
OFF_DK = OFF_DQ + DIFF_HEADS * 2 * DIFF_HD
OFF_DV = OFF_DK + DIFF_HEADS * 2 * DIFF_HD
OFF_GATE = OFF_DV + DIFF_HEADS * DIFF_VD

P_DIFF = 3 * 1024
P_MLA = 1536
P_COLS = P_DIFF + P_MLA + 2 * D_MODEL
P_TN = 512
MLA_BLK = P_DIFF // P_MLA
GATE_BLK = (P_DIFF + P_MLA) // P_TN

NEG = -1e30
ONES_ROWS = 16
VMEM_LIMIT = 56 * 1024 * 1024


def _cparams(sem):
    return pltpu.CompilerParams(dimension_semantics=sem, vmem_limit_bytes=VMEM_LIMIT)


def _rms(x, g):
    return x * lax.rsqrt(jnp.mean(x * x, axis=-1, keepdims=True) + EPS) * g


def _rope128(x, cos, sin_lo, sin_hi):
    return x * cos + pltpu.roll(x, 96, 1) * sin_lo + pltpu.roll(x, 32, 1) * sin_hi


def _proj_kernel(x_ref, g_ref, w_ref, o_ref, h_scr):
    @pl.when(pl.program_id(1) == 0)
    def _():
        h_scr[...] = _rms(x_ref[...], g_ref[...]).astype(BF16)

    o_ref[...] = jnp.dot(h_scr[...], w_ref[...], preferred_element_type=F32).astype(o_ref.dtype)


def _proj(x2d, g, w, tm):
    t, d = x2d.shape
    n = w.shape[1]
    return pl.pallas_call(
        _proj_kernel,
        grid=(t // tm, n // P_TN),
        in_specs=[pl.BlockSpec((tm, d), lambda i, j: (i, 0)),
                  pl.BlockSpec((1, d), lambda i, j: (0, 0)),
                  pl.BlockSpec((d, P_TN), lambda i, j: (0, j))],
        out_specs=pl.BlockSpec((tm, P_TN), lambda i, j: (i, j)),
        out_shape=jax.ShapeDtypeStruct((t, n), BF16),
        scratch_shapes=[pltpu.VMEM((tm, d), BF16)],
        compiler_params=_cparams(("parallel", "arbitrary")),
        name="proj",
    )(x2d, g, w)


def _diff_prep_kernel(q_ref, k_ref, v_ref, cos_ref, sl_ref, sh_ref, qt_ref, ko_ref, vt_ref, *, scale):
    cos, sl, sh = cos_ref[...], sl_ref[...], sh_ref[...]
    for h in range(DIFF_HEADS):
        c = slice(128 * h, 128 * h + 128)
        q = _rope128(q_ref[:, c].astype(F32), cos, sl, sh) * scale
        qt_ref[h] = q.T.astype(BF16)
        ko_ref[:, c] = _rope128(k_ref[:, c].astype(F32), cos, sl, sh).astype(BF16)
        vt_ref[h, 0:DIFF_VD, :] = v_ref[:, c].astype(F32).T.astype(BF16)
        vt_ref[h, DIFF_VD:DIFF_VD + ONES_ROWS, :] = jnp.ones((ONES_ROWS, q.shape[0]), BF16)


def _diff_prep(p, cos, sl, sh, batch, seq, ta):
    t = p.shape[0]
    w = DIFF_HEADS * 2 * DIFF_HD
    ns = seq // ta
    tab = pl.BlockSpec((ta, 128), lambda i: (i % ns, 0))
    tr = lambda rows: (jax.ShapeDtypeStruct((batch, DIFF_HEADS, ns, rows, ta), BF16),
                       pl.BlockSpec((None, DIFF_HEADS, None, rows, ta), lambda i: (i // ns, 0, i % ns, 0, 0)))
    qt_shape, qt_spec = tr(2 * DIFF_HD)
    vt_shape, vt_spec = tr(DIFF_VD + ONES_ROWS)
    return pl.pallas_call(
        functools.partial(_diff_prep_kernel, scale=DIFF_HD ** -0.5 * LOG2E),
        grid=(t // ta,),
        in_specs=[pl.BlockSpec((ta, w), lambda i: (i, 0)),
                  pl.BlockSpec((ta, w), lambda i: (i, 1)),
                  pl.BlockSpec((ta, w), lambda i: (i, 2)),
                  tab, tab, tab],
        out_specs=[qt_spec, pl.BlockSpec((ta, w), lambda i: (i, 0)), vt_spec],
        out_shape=[qt_shape, jax.ShapeDtypeStruct((t, w), BF16), vt_shape],
        compiler_params=_cparams(("parallel",)),
        name="diff_prep",
    )(p, p, p, cos, sl, sh)


def _mla_prep_kernel(p_ref, gq_ref, gkv_ref, wuq_ref, wuk_ref, wuv_ref, cos_ref, sl_ref, sh_ref,
                     qt_ref, k_ref, vt_ref, *, scale):
    cos, sl, sh = cos_ref[...], sl_ref[...], sh_ref[...]
    cq = _rms(p_ref[:, 0:MLA_Q_RANK].astype(F32), gq_ref[...]).astype(BF16)
    yq = jnp.dot(cq, wuq_ref[...], preferred_element_type=F32)
    ckv = _rms(p_ref[:, MLA_Q_RANK:MLA_Q_RANK + MLA_KV_RANK].astype(F32), gkv_ref[...]).astype(BF16)
    kn = jnp.dot(ckv, wuk_ref[...], preferred_element_type=F32)
    vv = jnp.dot(ckv, wuv_ref[...], preferred_element_type=F32)
    kr_off = MLA_Q_RANK + MLA_KV_RANK
    kr = _rope128(p_ref[:, kr_off:kr_off + 128].astype(F32), cos, sl, sh).astype(BF16)
    for h in range(MLA_HEADS):
        a = MLA_HEAD_PAD * h
        qt_ref[h, 0:128, :] = (yq[:, a:a + 128] * scale).T.astype(BF16)
        qt_ref[h, 128:256, :] = (_rope128(yq[:, a + 128:a + 256], cos, sl, sh) * scale).T.astype(BF16)
        k_ref[:, a:a + 128] = kn[:, 128 * h:128 * h + 128].astype(BF16)
        k_ref[:, a + 128:a + 256] = kr
        vt_ref[h, 0:MLA_V, :] = vv[:, 128 * h:128 * h + 128].T.astype(BF16)
        vt_ref[h, MLA_V:MLA_V + ONES_ROWS, :] = jnp.ones((ONES_ROWS, vv.shape[0]), BF16)


def _mla_prep(p, gq, gkv, wuq, wuk, wuv, cos, sl, sh, batch, seq, ta):
    t = p.shape[0]
    ns = seq // ta
    tab = pl.BlockSpec((ta, 128), lambda i: (i % ns, 0))
    full = lambda a: pl.BlockSpec(a.shape, lambda i: (0, 0))
    hw = MLA_HEADS * MLA_HEAD_PAD
    tr = lambda rows: (jax.ShapeDtypeStruct((batch, MLA_HEADS, ns, rows, ta), BF16),
                       pl.BlockSpec((None, MLA_HEADS, None, rows, ta), lambda i: (i // ns, 0, i % ns, 0, 0)))
    qt_shape, qt_spec = tr(MLA_HEAD_PAD)
    vt_shape, vt_spec = tr(MLA_V + ONES_ROWS)
    return pl.pallas_call(
        functools.partial(_mla_prep_kernel, scale=MLA_QK ** -0.5 * LOG2E),
        grid=(t // ta,),
        in_specs=[pl.BlockSpec((ta, P_MLA), lambda i: (i, MLA_BLK)),
                  full(gq), full(gkv), full(wuq), full(wuk), full(wuv), tab, tab, tab],
        out_specs=[qt_spec, pl.BlockSpec((ta, hw), lambda i: (i, 0)), vt_spec],
        out_shape=[qt_shape, jax.ShapeDtypeStruct((t, hw), BF16), vt_shape],
        compiler_params=_cparams(("parallel",)),
        name="mla_prep",
    )(p, gq, gkv, wuq, wuk, wuv, cos, sl, sh)


def _attn_maps(qts, k_ref, vt_ref, mask_ref, s_a, s_b, m_scrs, acc_scrs, ta, dv):
    n = pl.program_id(2)

    def scores(bufs, j):
        start = pl.multiple_of(j * ta, ta)
        kj = k_ref[pl.ds(start, ta), :]
        for buf, qt in zip(bufs, qts):
            buf[...] = jnp.dot(kj, qt, preferred_element_type=F32)

    def reduce(bufs, j, first=False):
        vt = vt_ref[j]
        for buf, m_scr, acc_scr in zip(bufs, m_scrs, acc_scrs):
            if first:
                s = buf[...] + mask_ref[...]
                m_new = jnp.max(s, axis=0, keepdims=True)
                p = jnp.exp2(s - m_new).astype(BF16)
                acc_scr[...] = jnp.dot(vt, p, preferred_element_type=F32)
            else:
                s = buf[...]
                m_old = m_scr[...]
                m_new = jnp.maximum(m_old, jnp.max(s, axis=0, keepdims=True))
                p = jnp.exp2(s - m_new).astype(BF16)
                acc_scr[...] = (jnp.exp2(m_old - m_new) * acc_scr[...]
                                + jnp.dot(vt, p, preferred_element_type=F32))
            m_scr[...] = m_new

    last = jnp.maximum(n - 1, 0)
    scores(s_a, n)
    scores(s_b, 0)
    reduce(s_a, n, first=True)

    def pair(pp, c):
        j0 = 2 * pp
        scores(s_a, j0 + 1)
        reduce(s_b, j0)
        scores(s_b, jnp.minimum(j0 + 2, last))
        reduce(s_a, j0 + 1)
        return c

    lax.fori_loop(0, n // 2, pair, 0)

    @pl.when(n % 2 == 1)
    def _():
        reduce(s_b, n - 1)

    return [acc_scr[0:dv] / acc_scr[dv:dv + 1] for acc_scr in acc_scrs]


def _mla_attn_kernel(qt_ref, k_ref, vt_ref, mask_ref, o_ref, sa, sb, m_scr, acc_scr, *, ta):
    (o,) = _attn_maps([qt_ref[...]], k_ref, vt_ref, mask_ref, [sa], [sb], [m_scr], [acc_scr], ta, MLA_V)
    o_ref[...] = o.T.astype(o_ref.dtype)


def _attn_scratch(nmaps, dv, ta):
    return ([pltpu.VMEM((ta, ta), F32)] * (2 * nmaps)
            + [pltpu.VMEM((1, ta), F32), pltpu.VMEM((dv + ONES_ROWS, ta), F32)] * nmaps)


def _mla_attn(qt, k, vt, mask, batch, seq, ta):
    t = k.shape[0]
    nq = seq // ta
    return pl.pallas_call(
        functools.partial(_mla_attn_kernel, ta=ta),
        grid=(batch, MLA_HEADS, nq),
        in_specs=[pl.BlockSpec((None, None, None, MLA_HEAD_PAD, ta), lambda b, h, i: (b, h, i, 0, 0)),
                  pl.BlockSpec((seq, MLA_HEAD_PAD), lambda b, h, i: (b, h)),
                  pl.BlockSpec((None, None, nq, MLA_V + ONES_ROWS, ta), lambda b, h, i: (b, h, 0, 0, 0)),
                  pl.BlockSpec(mask.shape, lambda b, h, i: (0, 0))],
        out_specs=pl.BlockSpec((ta, MLA_V), lambda b, h, i: (b * nq + i, h)),
        out_shape=jax.ShapeDtypeStruct((t, MLA_HEADS * MLA_V), BF16),
        scratch_shapes=_attn_scratch(1, MLA_V, ta),
        compiler_params=_cparams(("parallel", "parallel", "arbitrary")),
        name="mla_attn",
    )(qt, k, vt, mask)


def _diff_attn_kernel(qt_ref, k_ref, vt_ref, mask_ref, lq_ref, g_ref, o_ref,
                      sa1, sa2, sb1, sb2, m1, a1, m2, a2, *, ta, lambda_init):
    qt = qt_ref[...]
    row = lax.broadcasted_iota(jnp.int32, qt.shape, 0)
    zero = jnp.zeros_like(qt)
    qts = [jnp.where(row < DIFF_HD, qt, zero), jnp.where(row >= DIFF_HD, qt, zero)]
    o1, o2 = _attn_maps(qts, k_ref, vt_ref, mask_ref, [sa1, sa2], [sb1, sb2],
                        [m1, m2], [a1, a2], ta, DIFF_VD)
    lq = lq_ref[...]
    lam = (jnp.exp(jnp.sum(lq[0:1] * lq[1:2], axis=1, keepdims=True))
           - jnp.exp(jnp.sum(lq[2:3] * lq[3:4], axis=1, keepdims=True)) + lambda_init)
    o = (o1 - lam * o2).T
    o_ref[...] = (_rms(o, g_ref[...]) * (1.0 - lambda_init)).astype(o_ref.dtype)


def _diff_attn(qt, k, vt, mask, lq, g, batch, seq, ta, lambda_init):
    t = k.shape[0]
    nq = seq // ta
    return pl.pallas_call(
        functools.partial(_diff_attn_kernel, ta=ta, lambda_init=lambda_init),
        grid=(batch, DIFF_HEADS, nq),
        in_specs=[pl.BlockSpec((None, None, None, 128, ta), lambda b, h, i: (b, h, i, 0, 0)),
                  pl.BlockSpec((seq, 128), lambda b, h, i: (b, h)),
                  pl.BlockSpec((None, None, nq, DIFF_VD + ONES_ROWS, ta), lambda b, h, i: (b, h, 0, 0, 0)),
                  pl.BlockSpec(mask.shape, lambda b, h, i: (0, 0)),
                  pl.BlockSpec(lq.shape, lambda b, h, i: (0, 0)),
                  pl.BlockSpec(g.shape, lambda b, h, i: (0, 0))],
        out_specs=pl.BlockSpec((ta, DIFF_VD), lambda b, h, i: (b * nq + i, h)),
        out_shape=jax.ShapeDtypeStruct((t, DIFF_HEADS * DIFF_VD), BF16),
        scratch_shapes=_attn_scratch(2, DIFF_VD, ta),
        compiler_params=_cparams(("parallel", "parallel", "arbitrary")),
        name="diff_attn",
    )(qt, k, vt, mask, lq, g)


def _merge_kernel(om_ref, od_ref, ga_ref, gb_ref, ba_ref, bb_ref, wom_ref, wod_ref, wout_ref,
                  x_ref, gn_ref, x2_ref, ht_ref, acc_scr):
    j = pl.program_id(1)
    ya = jnp.dot(om_ref[...], wom_ref[...], preferred_element_type=F32)
    yb = jnp.dot(od_ref[...], wod_ref[...], preferred_element_type=F32)
    ga = jax.nn.sigmoid(ga_ref[...].astype(F32) + ba_ref[...])
    gb = jax.nn.sigmoid(gb_ref[...].astype(F32) + bb_ref[...])
    merged = (ga * ya + gb * yb).astype(BF16)
    contrib = jnp.dot(merged, wout_ref[...], preferred_element_type=F32)

    @pl.when(j == 0)
    def _():
        acc_scr[...] = x_ref[...] + contrib

    @pl.when(j > 0)
    def _():
        acc_scr[...] += contrib

    @pl.when(j == pl.num_programs(1) - 1)
    def _():
        x2 = acc_scr[...]
        x2_ref[...] = x2
        ht_ref[...] = _rms(x2, gn_ref[...]).T.astype(BF16)


def _merge(om, od, p, b_gate, wom, wod, wout, x2d, gn, tm):
    t, d = x2d.shape
    tn = P_TN
    nj = d // tn
    kin = om.shape[1]
    return pl.pallas_call(
        _merge_kernel,
        grid=(t // tm, nj),
        in_specs=[pl.BlockSpec((tm, kin), lambda i, j: (i, 0)),
                  pl.BlockSpec((tm, kin), lambda i, j: (i, 0)),
                  pl.BlockSpec((tm, tn), lambda i, j: (i, GATE_BLK + j)),
                  pl.BlockSpec((tm, tn), lambda i, j: (i, GATE_BLK + nj + j)),
                  pl.BlockSpec((1, tn), lambda i, j: (0, j)),
                  pl.BlockSpec((1, tn), lambda i, j: (0, nj + j)),
                  pl.BlockSpec((kin, tn), lambda i, j: (0, j)),
                  pl.BlockSpec((kin, tn), lambda i, j: (0, j)),
                  pl.BlockSpec((tn, d), lambda i, j: (j, 0)),
                  pl.BlockSpec((tm, d), lambda i, j: (i, 0)),
                  pl.BlockSpec((1, d), lambda i, j: (0, 0))],
        out_specs=[pl.BlockSpec((tm, d), lambda i, j: (i, 0)),
                   pl.BlockSpec((d, tm), lambda i, j: (0, i))],
        out_shape=[jax.ShapeDtypeStruct((t, d), F32), jax.ShapeDtypeStruct((d, t), BF16)],
        scratch_shapes=[pltpu.VMEM((tm, d), F32)],
        compiler_params=_cparams(("parallel", "arbitrary")),
        name="merge",
    )(om, od, p, p, b_gate, b_gate, wom, wod, wout, x2d, gn)


def _top16(s, iota_r):
    rank = jnp.full(s.shape, RANK_OUT, F32)
    vals = []
    for it in range(PEER_TOPK):
        m = jnp.max(s, axis=0, keepdims=True)
        first = jnp.min(jnp.where(s == m, iota_r, 999.0), axis=0, keepdims=True)
        sel = iota_r == first
        rank = jnp.where(sel, float(it), rank)
        s = jnp.where(sel, -jnp.inf, s)
        vals.append(m)
    return jnp.concatenate(vals, axis=0), rank


def _pair_select(v1, v2, iota_k):
    top = v1[0:1] + v2[0:1]
    cand = v1 + v2[0:1]
    cnt = jnp.zeros(v1.shape, F32)
    z = jnp.zeros(top.shape, F32)
    for _ in range(PEER_TOPK):
        m = jnp.max(cand, axis=0, keepdims=True)
        first = jnp.min(jnp.where(cand == m, iota_k, 999.0), axis=0, keepdims=True)
        sel = iota_k == first
        z = z + jnp.exp(m - top)
        cnt = jnp.where(sel, cnt + 1.0, cnt)
        taken = jnp.sum(jnp.where(sel, cnt, 0.0), axis=0, keepdims=True)
        v2n = jnp.sum(jnp.where(iota_k == taken, v2, 0.0), axis=0, keepdims=True)
        v1s = jnp.sum(jnp.where(sel, v1, 0.0), axis=0, keepdims=True)
        nxt = jnp.where(taken < float(PEER_TOPK), v1s + v2n, -jnp.inf)
        cand = jnp.where(sel, nxt, cand)
    return cnt, z


def _peer_topk_kernel(ht_ref, wqt_ref, sk_ref, cnt_ref, e1_ref, rank_ref, e2_ref, qt_scr):
    qt_scr[...] = jnp.dot(wqt_ref[...], ht_ref[...], preferred_element_type=F32)
    tn = ht_ref.shape[1]
    iota_r = lax.broadcasted_iota(jnp.int32, (PEER_NKEYS, tn), 0).astype(F32)
    iota_k = lax.broadcasted_iota(jnp.int32, (PEER_TOPK, tn), 0).astype(F32)

    def head(h, c):
        base = pl.multiple_of(h * 2 * PEER_HALF, 2 * PEER_HALF)
        q1 = qt_scr[pl.ds(base, PEER_HALF), :].astype(BF16)
        q2 = qt_scr[pl.ds(base + PEER_HALF, PEER_HALF), :].astype(BF16)
        s1 = jnp.dot(sk_ref[0], q1, preferred_element_type=F32)
        s2 = jnp.dot(sk_ref[1], q2, preferred_element_type=F32)
        v1, rank1 = _top16(s1, iota_r)
        v2, rank2 = _top16(s2, iota_r)
        cnt, z = _pair_select(v1, v2, iota_k)
        cnt_keys = jnp.zeros(s1.shape, F32)
        for i in range(PEER_TOPK):
            cnt_keys = jnp.where(rank1 == float(i), cnt[i:i + 1], cnt_keys)
        cnt_ref[h] = cnt_keys
        e1_ref[h] = jnp.exp(s1 - v1[0:1]) * (1.0 / z)
        rank_ref[h] = rank2.astype(BF16)
        e2_ref[h] = jnp.exp(s2 - v2[0:1]).astype(BF16)
        return c

    lax.fori_loop(0, PEER_HEADS, head, 0)


def _peer_topk(ht, wqt, sk, tn):
    d, t = ht.shape
    aux = lambda dt: jax.ShapeDtypeStruct((PEER_HEADS, PEER_NKEYS, t), dt)
    aux_spec = pl.BlockSpec((PEER_HEADS, PEER_NKEYS, tn), lambda i: (0, 0, i))
    return pl.pallas_call(
        _peer_topk_kernel,
        grid=(t // tn,),
        in_specs=[pl.BlockSpec((d, tn), lambda i: (0, i)),
                  pl.BlockSpec(wqt.shape, lambda i: (0, 0)),
                  pl.BlockSpec(sk.shape, lambda i: (0, 0, 0))],
        out_specs=[aux_spec, aux_spec, aux_spec, aux_spec],
        out_shape=[aux(F32), aux(F32), aux(BF16), aux(BF16)],
        scratch_shapes=[pltpu.VMEM((wqt.shape[0], tn), F32)],
        compiler_params=_cparams(("parallel",)),
        name="peer_topk",
    )(ht, wqt, sk)


def _peer_dense_kernel(u_ref, vt_ref, ht_ref, cnt_ref, e1_ref, rank_ref, e2_ref, o_ref, w_scr, *, rows):
    j = pl.program_id(1)
    at = jnp.dot(u_ref[...], ht_ref[...], preferred_element_type=F32)
    for r in range(rows):
        a = at[PEER_NKEYS * r:PEER_NKEYS * (r + 1)]
        g = jnp.zeros(a.shape, BF16)
        for h in range(PEER_HEADS):
            cnt = cnt_ref[h, r:r + 1, :].astype(BF16)
            e1 = e1_ref[h, r:r + 1, :].astype(BF16)
            g = g + jnp.where(rank_ref[h] < cnt, e2_ref[h] * e1, jnp.zeros_like(g))
        w_scr[PEER_NKEYS * r:PEER_NKEYS * (r + 1), :] = jax.nn.gelu(a).astype(BF16) * g
    contrib = jnp.dot(vt_ref[...], w_scr[...], preferred_element_type=F32)

    @pl.when(j == 0)
    def _():
        o_ref[...] = contrib

    @pl.when(j > 0)
    def _():
        o_ref[...] += contrib


def _peer_dense(u, vt, ht, cnt, e1, rank, e2, tn, rows):
    e, d = u.shape
    t = ht.shape[1]
    te = rows * PEER_NKEYS
    row_spec = pl.BlockSpec((PEER_HEADS, rows, tn), lambda i, j: (0, j, i))
    col_spec = pl.BlockSpec((PEER_HEADS, PEER_NKEYS, tn), lambda i, j: (0, 0, i))
    return pl.pallas_call(
        functools.partial(_peer_dense_kernel, rows=rows),
        grid=(t // tn, e // te),
        in_specs=[pl.BlockSpec((te, d), lambda i, j: (j, 0)),
                  pl.BlockSpec((d, te), lambda i, j: (0, j)),
                  pl.BlockSpec((d, tn), lambda i, j: (0, i)),
                  row_spec, row_spec, col_spec, col_spec],
        out_specs=pl.BlockSpec((d, tn), lambda i, j: (0, i)),
        out_shape=jax.ShapeDtypeStruct((d, t), F32),
        scratch_shapes=[pltpu.VMEM((te, tn), BF16)],
        compiler_params=_cparams(("parallel", "arbitrary")),
        name="peer_dense",
    )(u, vt, ht, cnt, e1, rank, e2)


def _final_kernel(x_ref, pt_ref, g_ref, o_ref):
    o_ref[...] = _rms(x_ref[...] + pt_ref[...].T, g_ref[...])


def _final(x2, pt, g, tm):
    t, d = x2.shape
    return pl.pallas_call(
        _final_kernel,
        grid=(t // tm,),
        in_specs=[pl.BlockSpec((tm, d), lambda i: (i, 0)),
                  pl.BlockSpec((d, tm), lambda i: (0, i)),
                  pl.BlockSpec((1, d), lambda i: (0, 0))],
        out_specs=pl.BlockSpec((tm, d), lambda i: (i, 0)),
        out_shape=jax.ShapeDtypeStruct((t, d), F32),
        compiler_params=_cparams(("parallel",)),
        name="final",
    )(x2, pt, g)


def _rope_tables(seq):
    half = MLA_ROPE // 2
    inv = ROPE_THETA ** (-jnp.arange(0, MLA_ROPE, 2, dtype=F32) / MLA_ROPE)
    ang = jnp.arange(seq, dtype=F32)[:, None] * inv[None, :]
    cos = jnp.tile(jnp.cos(ang), (1, 128 // half))
    sin = jnp.tile(jnp.sin(ang), (1, 128 // half))
    first_half = (jnp.arange(128) % MLA_ROPE) < half
    return cos, jnp.where(first_half, -sin, 0.0), jnp.where(first_half, 0.0, sin)


def _attn_mask(ta):
    key = jnp.arange(ta)[:, None]
    qry = jnp.arange(ta)[None, :]
    return jnp.where(key <= qry, 0.0, NEG).astype(F32)


def _tiles(seq):
    pick = lambda pref: max(c for c in (128, 256, 512, 1024) if c <= pref and seq % c == 0)
    return dict(proj=pick(1024), attn=pick(512), merge=pick(512),
                topk=pick(256), dense=pick(512), final=pick(512))


def kernel(x, w_in, b_gate, g_norm1, g_cq, w_uq, g_ckv, w_ukv, w_o_mla, lambda_qk, g_subln,
           w_o_diff, w_out, g_norm2, w_q_peer, sub_keys, expert_u, expert_v, g_final):
    batch, seq, d = x.shape
    assert d == D_MODEL and seq % 128 == 0
    assert w_in.shape[0] == 1, "single-layer trunk"
    l = 0
    lambda_init = 0.8 - 0.6 * math.exp(-0.3 * l)
    tl = _tiles(seq)
    ta = tl["attn"]
    cos, sl, sh = _rope_tables(seq)
    mask = _attn_mask(ta)
    x2d = x.reshape(batch * seq, d)
    row = lambda v: v.reshape(1, -1)

    wl = w_in[l]
    zpad = jnp.zeros((d, P_MLA - (MLA_Q_RANK + MLA_KV_RANK + MLA_ROPE)), F32)
    w_p = jnp.concatenate([wl[:, OFF_DQ:OFF_GATE], wl[:, OFF_CQ:OFF_DQ], zpad, wl[:, OFF_GATE:]],
                          axis=1).astype(BF16)
    wuq = w_uq[l].reshape(MLA_Q_RANK, MLA_HEADS, MLA_QK)
    wuq = jnp.pad(wuq, ((0, 0), (0, 0), (0, MLA_HEAD_PAD - MLA_QK)))
    wuq = wuq.reshape(MLA_Q_RANK, MLA_HEADS * MLA_HEAD_PAD).astype(BF16)
    wukv = w_ukv[l].reshape(MLA_KV_RANK, MLA_HEADS, MLA_NOPE + MLA_V)
    wuk = wukv[:, :, :MLA_NOPE].reshape(MLA_KV_RANK, MLA_HEADS * MLA_NOPE).astype(BF16)
    wuv = wukv[:, :, MLA_NOPE:].reshape(MLA_KV_RANK, MLA_HEADS * MLA_V).astype(BF16)

    p = _proj(x2d, row(g_norm1[l]), w_p, tl["proj"])
    qdt, kd, vdt = _diff_prep(p, cos, sl, sh, batch, seq, ta)
    qmt, km, vmt = _mla_prep(p, row(g_cq[l]), row(g_ckv[l]), wuq, wuk, wuv, cos, sl, sh, batch, seq, ta)
    om = _mla_attn(qmt, km, vmt, mask, batch, seq, ta)
    od = _diff_attn(qdt, kd, vdt, mask, lambda_qk[l], row(g_subln[l]), batch, seq, ta, lambda_init)
    x2, ht = _merge(om, od, p, row(b_gate[l]), w_o_mla[l].astype(BF16), w_o_diff[l].astype(BF16),
                    w_out[l].astype(BF16), x2d, row(g_norm2[l]), tl["merge"])
    cnt, e1, rank, e2 = _peer_topk(ht, w_q_peer[l].T.astype(BF16), sub_keys[l].astype(BF16), tl["topk"])
    pt = _peer_dense(expert_u[l].astype(BF16), expert_v[l].T.astype(BF16), ht, cnt, e1, rank, e2,
                     tl["dense"], 8)
    out = _final(x2, pt, row(g_final), tl["final"])
    return out.reshape(batch, seq, d)
```

```python
import functools
import math

import jax
import jax.numpy as jnp
from jax import lax
from jax.experimental import pallas as pl
from jax.experimental.pallas import tpu as pltpu

F32 = jnp.float32
BF16 = jnp.bfloat16

EPS = 1e-6
ROPE_THETA = 10000.0
D_MODEL = 2048
LOG2E = 1.4426950408889634

MLA_HEADS = 8
MLA_Q_RANK = 768
MLA_KV_RANK = 512
MLA_NOPE = 128
MLA_ROPE = 64
MLA_V = 128
MLA_QK = MLA_NOPE + MLA_ROPE
MLA_HEAD_PAD = 256

DIFF_HEADS = 8
DIFF_HD = 64
DIFF_VD = 2 * DIFF_HD

PEER_HEADS = 8
PEER_NKEYS = 128
PEER_EXPERTS = PEER_NKEYS * PEER_NKEYS
PEER_HALF = 128
PEER_TOPK = 16
RANK_OUT = 64.0

OFF_CQ = 0
OFF_CKV = OFF_CQ + MLA_Q_RANK
OFF_KR = OFF_CKV + MLA_KV_RANK
OFF_DQ = OFF_KR + MLA_ROPE
OFF_DK = OFF_DQ + DIFF_HEADS * 2 * DIFF_HD
OFF_DV = OFF_DK + DIFF_HEADS * 2 * DIFF_HD
OFF_GATE = OFF_DV + DIFF_HEADS * DIFF_VD

P_DIFF = 3 * 1024
P_MLA = 1536
P_COLS = P_DIFF + P_MLA + 2 * D_MODEL
P_TN = 512
MLA_BLK = P_DIFF // P_MLA
GATE_BLK = (P_DIFF + P_MLA) // P_TN

NEG = -1e30
ONES_ROWS = 16
VMEM_LIMIT = 56 * 1024 * 1024


def _cparams(sem):
    return pltpu.CompilerParams(dimension_semantics=sem, vmem_limit_bytes=VMEM_LIMIT)


def _rms(x, g):
    return x * lax.rsqrt(jnp.mean(x * x, axis=-1, keepdims=True) + EPS) * g


def _rope128(x, cos, sin_lo, sin_hi):
    return x * cos + pltpu.roll(x, 96, 1) * sin_lo + pltpu.roll(x, 32, 1) * sin_hi


def _proj_kernel(x_ref, g_ref, w_ref, o_ref, h_scr):
    @pl.when(pl.program_id(1) == 0)
    def _():
        h_scr[...] = _rms(x_ref[...], g_ref[...]).astype(BF16)

    o_ref[...] = jnp.dot(h_scr[...], w_ref[...], preferred_element_type=F32).astype(o_ref.dtype)


def _proj(x2d, g, w, tm):
    t, d = x2d.shape
    n = w.shape[1]
    return pl.pallas_call(
        _proj_kernel,
        grid=(t // tm, n // P_TN),
        in_specs=[pl.BlockSpec((tm, d), lambda i, j: (i, 0)),
                  pl.BlockSpec((1, d), lambda i, j: (0, 0)),
                  pl.BlockSpec((d, P_TN), lambda i, j: (0, j))],
        out_specs=pl.BlockSpec((tm, P_TN), lambda i, j: (i, j)),
        out_shape=jax.ShapeDtypeStruct((t, n), BF16),
        scratch_shapes=[pltpu.VMEM((tm, d), BF16)],
        compiler_params=_cparams(("parallel", "arbitrary")),
        name="proj",
    )(x2d, g, w)


def _diff_prep_kernel(q_ref, k_ref, v_ref, cos_ref, sl_ref, sh_ref, qt_ref, ko_ref, vt_ref, *, scale):
    cos, sl, sh = cos_ref[...], sl_ref[...], sh_ref[...]
    for h in range(DIFF_HEADS):
        c = slice(128 * h, 128 * h + 128)
        q = _rope128(q_ref[:, c].astype(F32), cos, sl, sh) * scale
        qt_ref[h] = q.T.astype(BF16)
        ko_ref[:, c] = _rope128(k_ref[:, c].astype(F32), cos, sl, sh).astype(BF16)
        vt_ref[h, 0:DIFF_VD, :] = v_ref[:, c].astype(F32).T.astype(BF16)
        vt_ref[h, DIFF_VD:DIFF_VD + ONES_ROWS, :] = jnp.ones((ONES_ROWS, q.shape[0]), BF16)


def _diff_prep(p, cos, sl, sh, batch, seq, ta):
    t = p.shape[0]
    w = DIFF_HEADS * 2 * DIFF_HD
    ns = seq // ta
    tab = pl.BlockSpec((ta, 128), lambda i: (i % ns, 0))
    tr = lambda rows: (jax.ShapeDtypeStruct((batch, DIFF_HEADS, ns, rows, ta), BF16),
                       pl.BlockSpec((None, DIFF_HEADS, None, rows, ta), lambda i: (i // ns, 0, i % ns, 0, 0)))
    qt_shape, qt_spec = tr(2 * DIFF_HD)
    vt_shape, vt_spec = tr(DIFF_VD + ONES_ROWS)
    return pl.pallas_call(
        functools.partial(_diff_prep_kernel, scale=DIFF_HD ** -0.5 * LOG2E),
        grid=(t // ta,),
        in_specs=[pl.BlockSpec((ta, w), lambda i: (i, 0)),
                  pl.BlockSpec((ta, w), lambda i: (i, 1)),
                  pl.BlockSpec((ta, w), lambda i: (i, 2)),
                  tab, tab, tab],
        out_specs=[qt_spec, pl.BlockSpec((ta, w), lambda i: (i, 0)), vt_spec],
        out_shape=[qt_shape, jax.ShapeDtypeStruct((t, w), BF16), vt_shape],
        compiler_params=_cparams(("parallel",)),
        name="diff_prep",
    )(p, p, p, cos, sl, sh)


def _mla_prep_kernel(p_ref, gq_ref, gkv_ref, wuq_ref, wuk_ref, wuv_ref, cos_ref, sl_ref, sh_ref,
                     qt_ref, k_ref, vt_ref, *, scale):
    cos, sl, sh = cos_ref[...], sl_ref[...], sh_ref[...]
    cq = _rms(p_ref[:, 0:MLA_Q_RANK].astype(F32), gq_ref[...]).astype(BF16)
    yq = jnp.dot(cq, wuq_ref[...], preferred_element_type=F32)
    ckv = _rms(p_ref[:, MLA_Q_RANK:MLA_Q_RANK + MLA_KV_RANK].astype(F32), gkv_ref[...]).astype(BF16)
    kn = jnp.dot(ckv, wuk_ref[...], preferred_element_type=F32)
    vv = jnp.dot(ckv, wuv_ref[...], preferred_element_type=F32)
    kr_off = MLA_Q_RANK + MLA_KV_RANK
    kr = _rope128(p_ref[:, kr_off:kr_off + 128].astype(F32), cos, sl, sh).astype(BF16)
    for h in range(MLA_HEADS):
        a = MLA_HEAD_PAD * h
        qt_ref[h, 0:128, :] = (yq[:, a:a + 128] * scale).T.astype(BF16)
        qt_ref[h, 128:256, :] = (_rope128(yq[:, a + 128:a + 256], cos, sl, sh) * scale).T.astype(BF16)
        k_ref[:, a:a + 128] = kn[:, 128 * h:128 * h + 128].astype(BF16)
        k_ref[:, a + 128:a + 256] = kr
        vt_ref[h, 0:MLA_V, :] = vv[:, 128 * h:128 * h + 128].T.astype(BF16)
        vt_ref[h, MLA_V:MLA_V + ONES_ROWS, :] = jnp.ones((ONES_ROWS, vv.shape[0]), BF16)


def _mla_prep(p, gq, gkv, wuq, wuk, wuv, cos, sl, sh, batch, seq, ta):
    t = p.shape[0]
    ns = seq // ta
    tab = pl.BlockSpec((ta, 128), lambda i: (i % ns, 0))
    full = lambda a: pl.BlockSpec(a.shape, lambda i: (0, 0))
    hw = MLA_HEADS * MLA_HEAD_PAD
    tr = lambda rows: (jax.ShapeDtypeStruct((batch, MLA_HEADS, ns, rows, ta), BF16),
                       pl.BlockSpec((None, MLA_HEADS, None, rows, ta), lambda i: (i // ns, 0, i % ns, 0, 0)))
    qt_shape, qt_spec = tr(MLA_HEAD_PAD)
    vt_shape, vt_spec = tr(MLA_V + ONES_ROWS)
    return pl.pallas_call(
        functools.partial(_mla_prep_kernel, scale=MLA_QK ** -0.5 * LOG2E),
        grid=(t // ta,),
        in_specs=[pl.BlockSpec((ta, P_MLA), lambda i: (i, MLA_BLK)),
                  full(gq), full(gkv), full(wuq), full(wuk), full(wuv), tab, tab, tab],
        out_specs=[qt_spec, pl.BlockSpec((ta, hw), lambda i: (i, 0)), vt_spec],
        out_shape=[qt_shape, jax.ShapeDtypeStruct((t, hw), BF16), vt_shape],
        compiler_params=_cparams(("parallel",)),
        name="mla_prep",
    )(p, gq, gkv, wuq, wuk, wuv, cos, sl, sh)


def _attn_maps(qts, k_ref, vt_ref, mask_ref, s_a, s_b, m_scrs, acc_scrs, ta, dv):
    n = pl.program_id(2)

    def scores(bufs, j):
        start = pl.multiple_of(j * ta, ta)
        kj = k_ref[pl.ds(start, ta), :]
        for buf, qt in zip(bufs, qts):
            buf[...] = jnp.dot(kj, qt, preferred_element_type=F32)

    def reduce(bufs, j, first=False):
        vt = vt_ref[j]
        for buf, m_scr, acc_scr in zip(bufs, m_scrs, acc_scrs):
            if first:
                s = buf[...] + mask_ref[...]
                m_new = jnp.max(s, axis=0, keepdims=True)
                p = jnp.exp2(s - m_new).astype(BF16)
                acc_scr[...] = jnp.dot(vt, p, preferred_element_type=F32)
            else:
                s = buf[...]
                m_old = m_scr[...]
                m_new = jnp.maximum(m_old, jnp.max(s, axis=0, keepdims=True))
                p = jnp.exp2(s - m_new).astype(BF16)
                acc_scr[...] = (jnp.exp2(m_old - m_new) * acc_scr[...]
                                + jnp.dot(vt, p, preferred_element_type=F32))
            m_scr[...] = m_new

    last = jnp.maximum(n - 1, 0)
    scores(s_a, n)
    scores(s_b, 0)
    reduce(s_a, n, first=True)

    def pair(pp, c):
        j0 = 2 * pp
        scores(s_a, j0 + 1)
        reduce(s_b, j0)
        scores(s_b, jnp.minimum(j0 + 2, last))
        reduce(s_a, j0 + 1)
        return c

    lax.fori_loop(0, n // 2, pair, 0)

    @pl.when(n % 2 == 1)
    def _():
        reduce(s_b, n - 1)

    return [acc_scr[0:dv] / acc_scr[dv:dv + 1] for acc_scr in acc_scrs]


def _mla_attn_kernel(qt_ref, k_ref, vt_ref, mask_ref, o_ref, sa, sb, m_scr, acc_scr, *, ta):
    (o,) = _attn_maps([qt_ref[...]], k_ref, vt_ref, mask_ref, [sa], [sb], [m_scr], [acc_scr], ta, MLA_V)
    o_ref[...] = o.T.astype(o_ref.dtype)


def _attn_scratch(nmaps, dv, ta):
    return ([pltpu.VMEM((ta, ta), F32)] * (2 * nmaps)
            + [pltpu.VMEM((1, ta), F32), pltpu.VMEM((dv + ONES_ROWS, ta), F32)] * nmaps)


def _mla_attn(qt, k, vt, mask, batch, seq, ta):
    t = k.shape[0]
    nq = seq // ta
    return pl.pallas_call(
        functools.partial(_mla_attn_kernel, ta=ta),
        grid=(batch, MLA_HEADS, nq),
        in_specs=[pl.BlockSpec((None, None, None, MLA_HEAD_PAD, ta), lambda b, h, i: (b, h, i, 0, 0)),
                  pl.BlockSpec((seq, MLA_HEAD_PAD), lambda b, h, i: (b, h)),
                  pl.BlockSpec((None, None, nq, MLA_V + ONES_ROWS, ta), lambda b, h, i: (b, h, 0, 0, 0)),
                  pl.BlockSpec(mask.shape, lambda b, h, i: (0, 0))],
        out_specs=pl.BlockSpec((ta, MLA_V), lambda b, h, i: (b * nq + i, h)),
        out_shape=jax.ShapeDtypeStruct((t, MLA_HEADS * MLA_V), BF16),
        scratch_shapes=_attn_scratch(1, MLA_V, ta),
        compiler_params=_cparams(("parallel", "parallel", "arbitrary")),
        name="mla_attn",
    )(qt, k, vt, mask)


def _diff_attn_kernel(qt_ref, k_ref, vt_ref, mask_ref, lq_ref, g_ref, o_ref,
                      sa1, sa2, sb1, sb2, m1, a1, m2, a2, *, ta, lambda_init):
    qt = qt_ref[...]
    row = lax.broadcasted_iota(jnp.int32, qt.shape, 0)
    zero = jnp.zeros_like(qt)
    qts = [jnp.where(row < DIFF_HD, qt, zero), jnp.where(row >= DIFF_HD, qt, zero)]
    o1, o2 = _attn_maps(qts, k_ref, vt_ref, mask_ref, [sa1, sa2], [sb1, sb2],
                        [m1, m2], [a1, a2], ta, DIFF_VD)
    lq = lq_ref[...]
    lam = (jnp.exp(jnp.sum(lq[0:1] * lq[1:2], axis=1, keepdims=True))
           - jnp.exp(jnp.sum(lq[2:3] * lq[3:4], axis=1, keepdims=True)) + lambda_init)
    o = (o1 - lam * o2).T
    o_ref[...] = (_rms(o, g_ref[...]) * (1.0 - lambda_init)).astype(o_ref.dtype)


def _diff_attn(qt, k, vt, mask, lq, g, batch, seq, ta, lambda_init):
    t = k.shape[0]
    nq = seq // ta
    return pl.pallas_call(
        functools.partial(_diff_attn_kernel, ta=ta, lambda_init=lambda_init),
        grid=(batch, DIFF_HEADS, nq),
        in_specs=[pl.BlockSpec((None, None, None, 128, ta), lambda b, h, i: (b, h, i, 0, 0)),
                  pl.BlockSpec((seq, 128), lambda b, h, i: (b, h)),
                  pl.BlockSpec((None, None, nq, DIFF_VD + ONES_ROWS, ta), lambda b, h, i: (b, h, 0, 0, 0)),
                  pl.BlockSpec(mask.shape, lambda b, h, i: (0, 0)),
                  pl.BlockSpec(lq.shape, lambda b, h, i: (0, 0)),
                  pl.BlockSpec(g.shape, lambda b, h, i: (0, 0))],
        out_specs=pl.BlockSpec((ta, DIFF_VD), lambda b, h, i: (b * nq + i, h)),
        out_shape=jax.ShapeDtypeStruct((t, DIFF_HEADS * DIFF_VD), BF16),
        scratch_shapes=_attn_scratch(2, DIFF_VD, ta),
        compiler_params=_cparams(("parallel", "parallel", "arbitrary")),
        name="diff_attn",
    )(qt, k, vt, mask, lq, g)


def _merge_kernel(om_ref, od_ref, ga_ref, gb_ref, ba_ref, bb_ref, wom_ref, wod_ref, wout_ref,
                  x_ref, gn_ref, x2_ref, ht_ref, acc_scr):
    j = pl.program_id(1)
    ya = jnp.dot(om_ref[...], wom_ref[...], preferred_element_type=F32)
    yb = jnp.dot(od_ref[...], wod_ref[...], preferred_element_type=F32)
    ga = jax.nn.sigmoid(ga_ref[...].astype(F32) + ba_ref[...])
    gb = jax.nn.sigmoid(gb_ref[...].astype(F32) + bb_ref[...])
    merged = (ga * ya + gb * yb).astype(BF16)
    contrib = jnp.dot(merged, wout_ref[...], preferred_element_type=F32)

    @pl.when(j == 0)
    def _():
        acc_scr[...] = x_ref[...] + contrib

    @pl.when(j > 0)
    def _():
        acc_scr[...] += contrib

    @pl.when(j == pl.num_programs(1) - 1)
    def _():
        x2 = acc_scr[...]
        x2_ref[...] = x2
        ht_ref[...] = _rms(x2, gn_ref[...]).T.astype(BF16)


def _merge(om, od, p, b_gate, wom, wod, wout, x2d, gn, tm):
    t, d = x2d.shape
    tn = P_TN
    nj = d // tn
    kin = om.shape[1]
    return pl.pallas_call(
        _merge_kernel,
        grid=(t // tm, nj),
        in_specs=[pl.BlockSpec((tm, kin), lambda i, j: (i, 0)),
                  pl.BlockSpec((tm, kin), lambda i, j: (i, 0)),
                  pl.BlockSpec((tm, tn), lambda i, j: (i, GATE_BLK + j)),
                  pl.BlockSpec((tm, tn), lambda i, j: (i, GATE_BLK + nj + j)),
                  pl.BlockSpec((1, tn), lambda i, j: (0, j)),
                  pl.BlockSpec((1, tn), lambda i, j: (0, nj + j)),
                  pl.BlockSpec((kin, tn), lambda i, j: (0, j)),
                  pl.BlockSpec((kin, tn), lambda i, j: (0, j)),
                  pl.BlockSpec((tn, d), lambda i, j: (j, 0)),
                  pl.BlockSpec((tm, d), lambda i, j: (i, 0)),
                  pl.BlockSpec((1, d), lambda i, j: (0, 0))],
        out_specs=[pl.BlockSpec((tm, d), lambda i, j: (i, 0)),
                   pl.BlockSpec((d, tm), lambda i, j: (0, i))],
        out_shape=[jax.ShapeDtypeStruct((t, d), F32), jax.ShapeDtypeStruct((d, t), BF16)],
        scratch_shapes=[pltpu.VMEM((tm, d), F32)],
        compiler_params=_cparams(("parallel", "arbitrary")),
        name="merge",
    )(om, od, p, p, b_gate, b_gate, wom, wod, wout, x2d, gn)


def _top16(s, iota_r, exact):
    rank = jnp.full(s.shape, RANK_OUT, F32)
    vals = []
    for it in range(PEER_TOPK):
        m = jnp.max(s, axis=0, keepdims=True)
        sel = s == m
        if exact:
            first = jnp.min(jnp.where(sel, iota_r, 999.0), axis=0, keepdims=True)
            sel = iota_r == first
        rank = jnp.where(sel, float(it), rank)
        s = jnp.where(sel, -jnp.inf, s)
        vals.append(m)
    taken = jnp.sum(jnp.where(rank < RANK_OUT, 1.0, 0.0), axis=0, keepdims=True)
    return jnp.concatenate(vals, axis=0), rank, taken


def _pair_select(v1, v2, iota_k):
    top = v1[0:1] + v2[0:1]
    cand = v1 + v2[0:1]
    cnt = jnp.zeros(v1.shape, F32)
    z = jnp.zeros(top.shape, F32)
    for _ in range(PEER_TOPK):
        m = jnp.max(cand, axis=0, keepdims=True)
        first = jnp.min(jnp.where(cand == m, iota_k, 999.0), axis=0, keepdims=True)
        sel = iota_k == first
        z = z + jnp.exp(m - top)
        cnt = jnp.where(sel, cnt + 1.0, cnt)
        taken = jnp.sum(jnp.where(sel, cnt, 0.0), axis=0, keepdims=True)
        v2n = jnp.sum(jnp.where(iota_k == taken, v2, 0.0), axis=0, keepdims=True)
        v1s = jnp.sum(jnp.where(sel, v1, 0.0), axis=0, keepdims=True)
        nxt = jnp.where(taken < float(PEER_TOPK), v1s + v2n, -jnp.inf)
        cand = jnp.where(sel, nxt, cand)
    return cnt, z


def _peer_topk_kernel(ht_ref, wqt_ref, sk_ref, cnt_ref, e1_ref, rank_ref, e2_ref, qt_scr):
    qt_scr[...] = jnp.dot(wqt_ref[...], ht_ref[...], preferred_element_type=F32)
    tn = ht_ref.shape[1]
    iota_r = lax.broadcasted_iota(jnp.int32, (PEER_NKEYS, tn), 0).astype(F32)
    iota_k = lax.broadcasted_iota(jnp.int32, (PEER_TOPK, tn), 0).astype(F32)

    def head(h, c):
        base = pl.multiple_of(h * 2 * PEER_HALF, 2 * PEER_HALF)
        q1 = qt_scr[pl.ds(base, PEER_HALF), :].astype(BF16)
        q2 = qt_scr[pl.ds(base + PEER_HALF, PEER_HALF), :].astype(BF16)
        s1 = jnp.dot(sk_ref[0], q1, preferred_element_type=F32)
        s2 = jnp.dot(sk_ref[1], q2, preferred_element_type=F32)
        def select(exact):
            v1, rank1, n1 = _top16(s1, iota_r, exact)
            v2, rank2, n2 = _top16(s2, iota_r, exact)
            cnt, z = _pair_select(v1, v2, iota_k)
            cnt_keys = jnp.zeros(s1.shape, F32)
            for i in range(PEER_TOPK):
                cnt_keys = jnp.where(rank1 == float(i), cnt[i:i + 1], cnt_keys)
            cnt_ref[h] = cnt_keys
            e1_ref[h] = jnp.exp(s1 - v1[0:1]) * (1.0 / z)
            rank_ref[h] = rank2.astype(BF16)
            e2_ref[h] = jnp.exp(s2 - v2[0:1]).astype(BF16)
            return jnp.max(jnp.maximum(n1, n2))

        most_taken = select(exact=False)

        @pl.when(most_taken > float(PEER_TOPK))
        def _():
            select(exact=True)

        return c

    lax.fori_loop(0, PEER_HEADS, head, 0)


def _peer_topk(ht, wqt, sk, tn):
    d, t = ht.shape
    aux = lambda dt: jax.ShapeDtypeStruct((PEER_HEADS, PEER_NKEYS, t), dt)
    aux_spec = pl.BlockSpec((PEER_HEADS, PEER_NKEYS, tn), lambda i: (0, 0, i))
    return pl.pallas_call(
        _peer_topk_kernel,
        grid=(t // tn,),
        in_specs=[pl.BlockSpec((d, tn), lambda i: (0, i)),
                  pl.BlockSpec(wqt.shape, lambda i: (0, 0)),
                  pl.BlockSpec(sk.shape, lambda i: (0, 0, 0))],
        out_specs=[aux_spec, aux_spec, aux_spec, aux_spec],
        out_shape=[aux(F32), aux(F32), aux(BF16), aux(BF16)],
        scratch_shapes=[pltpu.VMEM((wqt.shape[0], tn), F32)],
        compiler_params=_cparams(("parallel",)),
        name="peer_topk",
    )(ht, wqt, sk)


def _peer_dense_kernel(u_ref, vta_ref, vtb_ref, ht_ref, cnt_ref, e1_ref, rank_ref, e2_ref, o_ref,
                       wa_scr, wb_scr, *, rows, nj):
    j = pl.program_id(1)
    nk = PEER_NKEYS

    def gates(r):
        g = jnp.zeros((nk, ht_ref.shape[1]), BF16)
        for h in range(PEER_HEADS):
            cnt = cnt_ref[h, r:r + 1, :].astype(BF16)
            e1 = e1_ref[h, r:r + 1, :].astype(BF16)
            g = g + jnp.where(rank_ref[h] < cnt, e2_ref[h] * e1, jnp.zeros_like(g))
        return g

    half = rows // 2
    hk = half * nk

    def weights(lo):
        g = jnp.concatenate([gates(r) for r in range(lo, lo + half)], axis=0)
        at = jnp.dot(u_ref[nk * lo:nk * lo + hk, :], ht_ref[...], preferred_element_type=F32)
        return jax.nn.gelu(at).astype(BF16) * g

    @pl.when(j == 0)
    def _():
        o_ref[...] = jnp.zeros(o_ref.shape, F32)
        wb_scr[...] = jnp.zeros(wb_scr.shape, BF16)

    @pl.when(j < nj)
    def _():
        wa = weights(0)
        o_ref[...] += jnp.dot(vtb_ref[...], wb_scr[...], preferred_element_type=F32)
        wa_scr[...] = wa
        wb = weights(half)
        o_ref[...] += jnp.dot(vta_ref[...], wa_scr[...], preferred_element_type=F32)
        wb_scr[...] = wb

    @pl.when(j == nj)
    def _():
        o_ref[...] += jnp.dot(vtb_ref[...], wb_scr[...], preferred_element_type=F32)


def _peer_dense(u, vt, ht, cnt, e1, rank, e2, tn, rows):
    e, d = u.shape
    t = ht.shape[1]
    te = rows * PEER_NKEYS
    nj = e // te
    cur = lambda j: jnp.minimum(j, nj - 1)
    row_spec = pl.BlockSpec((PEER_HEADS, rows, tn), lambda i, j: (0, cur(j), i))
    col_spec = pl.BlockSpec((PEER_HEADS, PEER_NKEYS, tn), lambda i, j: (0, 0, i))
    return pl.pallas_call(
        functools.partial(_peer_dense_kernel, rows=rows, nj=nj),
        grid=(t // tn, nj + 1),
        in_specs=[pl.BlockSpec((te, d), lambda i, j: (cur(j), 0)),
                  pl.BlockSpec((d, te // 2), lambda i, j: (0, 2 * cur(j))),
                  pl.BlockSpec((d, te // 2), lambda i, j: (0, jnp.maximum(2 * j - 1, 0))),
                  pl.BlockSpec((d, tn), lambda i, j: (0, i)),
                  row_spec, row_spec, col_spec, col_spec],
        out_specs=pl.BlockSpec((d, tn), lambda i, j: (0, i)),
        out_shape=jax.ShapeDtypeStruct((d, t), F32),
        scratch_shapes=[pltpu.VMEM((te // 2, tn), BF16), pltpu.VMEM((te // 2, tn), BF16)],
        compiler_params=_cparams(("parallel", "arbitrary")),
        name="peer_dense",
    )(u, vt, vt, ht, cnt, e1, rank, e2)


def _final_kernel(x_ref, pt_ref, g_ref, o_ref):
    o_ref[...] = _rms(x_ref[...] + pt_ref[...].T, g_ref[...])


def _final(x2, pt, g, tm):
    t, d = x2.shape
    return pl.pallas_call(
        _final_kernel,
        grid=(t // tm,),
        in_specs=[pl.BlockSpec((tm, d), lambda i: (i, 0)),
                  pl.BlockSpec((d, tm), lambda i: (0, i)),
                  pl.BlockSpec((1, d), lambda i: (0, 0))],
        out_specs=pl.BlockSpec((tm, d), lambda i: (i, 0)),
        out_shape=jax.ShapeDtypeStruct((t, d), F32),
        compiler_params=_cparams(("parallel",)),
        name="final",
    )(x2, pt, g)


def _rope_tables(seq):
    half = MLA_ROPE // 2
    inv = ROPE_THETA ** (-jnp.arange(0, MLA_ROPE, 2, dtype=F32) / MLA_ROPE)
    ang = jnp.arange(seq, dtype=F32)[:, None] * inv[None, :]
    cos = jnp.tile(jnp.cos(ang), (1, 128 // half))
    sin = jnp.tile(jnp.sin(ang), (1, 128 // half))
    first_half = (jnp.arange(128) % MLA_ROPE) < half
    return cos, jnp.where(first_half, -sin, 0.0), jnp.where(first_half, 0.0, sin)


def _attn_mask(ta):
    key = jnp.arange(ta)[:, None]
    qry = jnp.arange(ta)[None, :]
    return jnp.where(key <= qry, 0.0, NEG).astype(F32)


def _tiles(seq):
    pick = lambda pref: max(c for c in (128, 256, 512, 1024) if c <= pref and seq % c == 0)
    return dict(proj=pick(1024), attn=pick(512), merge=pick(512),
                topk=pick(256), dense=pick(512), final=pick(512))


def kernel(x, w_in, b_gate, g_norm1, g_cq, w_uq, g_ckv, w_ukv, w_o_mla, lambda_qk, g_subln,
           w_o_diff, w_out, g_norm2, w_q_peer, sub_keys, expert_u, expert_v, g_final):
    batch, seq, d = x.shape
    assert d == D_MODEL and seq % 128 == 0
    assert w_in.shape[0] == 1, "single-layer trunk"
    l = 0
    lambda_init = 0.8 - 0.6 * math.exp(-0.3 * l)
    tl = _tiles(seq)
    ta = tl["attn"]
    cos, sl, sh = _rope_tables(seq)
    mask = _attn_mask(ta)
    x2d = x.reshape(batch * seq, d)
    row = lambda v: v.reshape(1, -1)

    wl = w_in[l]
    zpad = jnp.zeros((d, P_MLA - (MLA_Q_RANK + MLA_KV_RANK + MLA_ROPE)), F32)
    w_p = jnp.concatenate([wl[:, OFF_DQ:OFF_GATE], wl[:, OFF_CQ:OFF_DQ], zpad, wl[:, OFF_GATE:]],
                          axis=1).astype(BF16)
    wuq = w_uq[l].reshape(MLA_Q_RANK, MLA_HEADS, MLA_QK)
    wuq = jnp.pad(wuq, ((0, 0), (0, 0), (0, MLA_HEAD_PAD - MLA_QK)))
    wuq = wuq.reshape(MLA_Q_RANK, MLA_HEADS * MLA_HEAD_PAD).astype(BF16)
    wukv = w_ukv[l].reshape(MLA_KV_RANK, MLA_HEADS, MLA_NOPE + MLA_V)
    wuk = wukv[:, :, :MLA_NOPE].reshape(MLA_KV_RANK, MLA_HEADS * MLA_NOPE).astype(BF16)
    wuv = wukv[:, :, MLA_NOPE:].reshape(MLA_KV_RANK, MLA_HEADS * MLA_V).astype(BF16)

    p = _proj(x2d, row(g_norm1[l]), w_p, tl["proj"])
    qdt, kd, vdt = _diff_prep(p, cos, sl, sh, batch, seq, ta)
    qmt, km, vmt = _mla_prep(p, row(g_cq[l]), row(g_ckv[l]), wuq, wuk, wuv, cos, sl, sh, batch, seq, ta)
    om = _mla_attn(qmt, km, vmt, mask, batch, seq, ta)
    od = _diff_attn(qdt, kd, vdt, mask, lambda_qk[l], row(g_subln[l]), batch, seq, ta, lambda_init)
    x2, ht = _merge(om, od, p, row(b_gate[l]), w_o_mla[l].astype(BF16), w_o_diff[l].astype(BF16),
                    w_out[l].astype(BF16), x2d, row(g_norm2[l]), tl["merge"])
    cnt, e1, rank, e2 = _peer_topk(ht, w_q_peer[l].T.astype(BF16), sub_keys[l].astype(BF16), tl["topk"])
    pt = _peer_dense(expert_u[l].astype(BF16), expert_v[l].T.astype(BF16), ht, cnt, e1, rank, e2,
                     tl["dense"], 8)
    out = _final(x2, pt, row(g_final), tl["final"])
    return out.reshape(batch, seq, d)
```

```python
import functools
import math

import jax
import jax.numpy as jnp
from jax import lax
from jax.experimental import pallas as pl
from jax.experimental.pallas import tpu as pltpu

F32 = jnp.float32
BF16 = jnp.bfloat16

EPS = 1e-6
ROPE_THETA = 10000.0
D_MODEL = 2048
LOG2E = 1.4426950408889634

MLA_HEADS = 8
MLA_Q_RANK = 768
MLA_KV_RANK = 512
MLA_NOPE = 128
MLA_ROPE = 64
MLA_V = 128
MLA_QK = MLA_NOPE + MLA_ROPE
MLA_HEAD_PAD = 256

DIFF_HEADS = 8
DIFF_HD = 64
DIFF_VD = 2 * DIFF_HD

PEER_HEADS = 8
PEER_NKEYS = 128
PEER_EXPERTS = PEER_NKEYS * PEER_NKEYS
PEER_HALF = 128
PEER_TOPK = 16
RANK_OUT = 64.0

OFF_CQ = 0
OFF_CKV = OFF_CQ + MLA_Q_RANK
OFF_KR = OFF_CKV + MLA_KV_RANK
OFF_DQ = OFF_KR + MLA_ROPE
OFF_DK = OFF_DQ + DIFF_HEADS * 2 * DIFF_HD
OFF_DV = OFF_DK + DIFF_HEADS * 2 * DIFF_HD
OFF_GATE = OFF_DV + DIFF_HEADS * DIFF_VD

P_DIFF = 3 * 1024
P_MLA = 1536
P_COLS = P_DIFF + P_MLA + 2 * D_MODEL
P_TN = 512
MLA_BLK = P_DIFF // P_MLA
GATE_BLK = (P_DIFF + P_MLA) // P_TN

NEG = -1e30
ONES_ROWS = 16
VMEM_LIMIT = 56 * 1024 * 1024


def _cparams(sem):
    return pltpu.CompilerParams(dimension_semantics=sem, vmem_limit_bytes=VMEM_LIMIT)


def _rms(x, g):
    return x * lax.rsqrt(jnp.mean(x * x, axis=-1, keepdims=True) + EPS) * g


def _rope128(x, cos, sin_lo, sin_hi):
    return x * cos + pltpu.roll(x, 96, 1) * sin_lo + pltpu.roll(x, 32, 1) * sin_hi


def _proj_kernel(x_ref, g_ref, w_ref, o_ref, h_scr):
    @pl.when(pl.program_id(1) == 0)
    def _():
        h_scr[...] = _rms(x_ref[...], g_ref[...]).astype(BF16)

    o_ref[...] = jnp.dot(h_scr[...], w_ref[...], preferred_element_type=F32).astype(o_ref.dtype)


def _proj(x2d, g, w, tm):
    t, d = x2d.shape
    n = w.shape[1]
    return pl.pallas_call(
        _proj_kernel,
        grid=(t // tm, n // P_TN),
        in_specs=[pl.BlockSpec((tm, d), lambda i, j: (i, 0)),
                  pl.BlockSpec((1, d), lambda i, j: (0, 0)),
                  pl.BlockSpec((d, P_TN), lambda i, j: (0, j))],
        out_specs=pl.BlockSpec((tm, P_TN), lambda i, j: (i, j)),
        out_shape=jax.ShapeDtypeStruct((t, n), BF16),
        scratch_shapes=[pltpu.VMEM((tm, d), BF16)],
        compiler_params=_cparams(("parallel", "arbitrary")),
        name="proj",
    )(x2d, g, w)


def _diff_prep_kernel(q_ref, k_ref, v_ref, cos_ref, sl_ref, sh_ref, qt_ref, ko_ref, vt_ref, *, scale):
    cos, sl, sh = cos_ref[...], sl_ref[...], sh_ref[...]
    for h in range(DIFF_HEADS):
        c = slice(128 * h, 128 * h + 128)
        q = _rope128(q_ref[:, c].astype(F32), cos, sl, sh) * scale
        qt_ref[h] = q.T.astype(BF16)
        ko_ref[:, c] = _rope128(k_ref[:, c].astype(F32), cos, sl, sh).astype(BF16)
        vt_ref[h, 0:DIFF_VD, :] = v_ref[:, c].astype(F32).T.astype(BF16)
        vt_ref[h, DIFF_VD:DIFF_VD + ONES_ROWS, :] = jnp.ones((ONES_ROWS, q.shape[0]), BF16)


def _diff_prep(p, cos, sl, sh, batch, seq, ta):
    t = p.shape[0]
    w = DIFF_HEADS * 2 * DIFF_HD
    ns = seq // ta
    tab = pl.BlockSpec((ta, 128), lambda i: (i % ns, 0))
    tr = lambda rows: (jax.ShapeDtypeStruct((batch, DIFF_HEADS, ns, rows, ta), BF16),
                       pl.BlockSpec((None, DIFF_HEADS, None, rows, ta), lambda i: (i // ns, 0, i % ns, 0, 0)))
    qt_shape, qt_spec = tr(2 * DIFF_HD)
    vt_shape, vt_spec = tr(DIFF_VD + ONES_ROWS)
    return pl.pallas_call(
        functools.partial(_diff_prep_kernel, scale=DIFF_HD ** -0.5 * LOG2E),
        grid=(t // ta,),
        in_specs=[pl.BlockSpec((ta, w), lambda i: (i, 0)),
                  pl.BlockSpec((ta, w), lambda i: (i, 1)),
                  pl.BlockSpec((ta, w), lambda i: (i, 2)),
                  tab, tab, tab],
        out_specs=[qt_spec, pl.BlockSpec((ta, w), lambda i: (i, 0)), vt_spec],
        out_shape=[qt_shape, jax.ShapeDtypeStruct((t, w), BF16), vt_shape],
        compiler_params=_cparams(("parallel",)),
        name="diff_prep",
    )(p, p, p, cos, sl, sh)


def _mla_prep_kernel(p_ref, gq_ref, gkv_ref, wuq_ref, wuk_ref, wuv_ref, cos_ref, sl_ref, sh_ref,
                     qt_ref, k_ref, vt_ref, *, scale):
    cos, sl, sh = cos_ref[...], sl_ref[...], sh_ref[...]
    cq = _rms(p_ref[:, 0:MLA_Q_RANK].astype(F32), gq_ref[...]).astype(BF16)
    yq = jnp.dot(cq, wuq_ref[...], preferred_element_type=F32)
    ckv = _rms(p_ref[:, MLA_Q_RANK:MLA_Q_RANK + MLA_KV_RANK].astype(F32), gkv_ref[...]).astype(BF16)
    kn = jnp.dot(ckv, wuk_ref[...], preferred_element_type=F32)
    vv = jnp.dot(ckv, wuv_ref[...], preferred_element_type=F32)
    kr_off = MLA_Q_RANK + MLA_KV_RANK
    kr = _rope128(p_ref[:, kr_off:kr_off + 128].astype(F32), cos, sl, sh).astype(BF16)
    for h in range(MLA_HEADS):
        a = MLA_HEAD_PAD * h
        qt_ref[h, 0:128, :] = (yq[:, a:a + 128] * scale).T.astype(BF16)
        qt_ref[h, 128:256, :] = (_rope128(yq[:, a + 128:a + 256], cos, sl, sh) * scale).T.astype(BF16)
        k_ref[:, a:a + 128] = kn[:, 128 * h:128 * h + 128].astype(BF16)
        k_ref[:, a + 128:a + 256] = kr
        vt_ref[h, 0:MLA_V, :] = vv[:, 128 * h:128 * h + 128].T.astype(BF16)
        vt_ref[h, MLA_V:MLA_V + ONES_ROWS, :] = jnp.ones((ONES_ROWS, vv.shape[0]), BF16)


def _mla_prep(p, gq, gkv, wuq, wuk, wuv, cos, sl, sh, batch, seq, ta):
    t = p.shape[0]
    ns = seq // ta
    tab = pl.BlockSpec((ta, 128), lambda i: (i % ns, 0))
    full = lambda a: pl.BlockSpec(a.shape, lambda i: (0, 0))
    hw = MLA_HEADS * MLA_HEAD_PAD
    tr = lambda rows: (jax.ShapeDtypeStruct((batch, MLA_HEADS, ns, rows, ta), BF16),
                       pl.BlockSpec((None, MLA_HEADS, None, rows, ta), lambda i: (i // ns, 0, i % ns, 0, 0)))
    qt_shape, qt_spec = tr(MLA_HEAD_PAD)
    vt_shape, vt_spec = tr(MLA_V + ONES_ROWS)
    return pl.pallas_call(
        functools.partial(_mla_prep_kernel, scale=MLA_QK ** -0.5 * LOG2E),
        grid=(t // ta,),
        in_specs=[pl.BlockSpec((ta, P_MLA), lambda i: (i, MLA_BLK)),
                  full(gq), full(gkv), full(wuq), full(wuk), full(wuv), tab, tab, tab],
        out_specs=[qt_spec, pl.BlockSpec((ta, hw), lambda i: (i, 0)), vt_spec],
        out_shape=[qt_shape, jax.ShapeDtypeStruct((t, hw), BF16), vt_shape],
        compiler_params=_cparams(("parallel",)),
        name="mla_prep",
    )(p, gq, gkv, wuq, wuk, wuv, cos, sl, sh)


def _attn_maps(qts, k_ref, vt_ref, mask_ref, s_a, s_b, m_scrs, acc_scrs, ta, dv):
    n = 2 * pl.program_id(2)

    def scores(bufs, j):
        start = pl.multiple_of(j * ta, ta)
        kj = k_ref[pl.ds(start, ta), :]
        for buf, qt in zip(bufs, qts):
            buf[...] = jnp.dot(kj, qt, preferred_element_type=F32)

    def reduce(bufs, j, mask=None, first=False):
        vt = vt_ref[j]
        for buf, m_scr, acc_scr in zip(bufs, m_scrs, acc_scrs):
            s = buf[...] if mask is None else buf[...] + mask_ref[mask]
            if first:
                m_new = jnp.max(s, axis=0, keepdims=True)
                p = jnp.exp2(s - m_new).astype(BF16)
                acc_scr[...] = jnp.dot(vt, p, preferred_element_type=F32)
            else:
                m_old = m_scr[...]
                m_new = jnp.maximum(m_old, jnp.max(s, axis=0, keepdims=True))
                p = jnp.exp2(s - m_new).astype(BF16)
                acc_scr[...] = (jnp.exp2(m_old - m_new) * acc_scr[...]
                                + jnp.dot(vt, p, preferred_element_type=F32))
            m_scr[...] = m_new

    last = jnp.maximum(n - 1, 0)
    scores(s_a, n)
    scores(s_b, n + 1)
    reduce(s_a, n, mask=0, first=True)
    scores(s_a, 0)
    reduce(s_b, n + 1, mask=1)

    def pair(pp, c):
        j0 = 2 * pp
        scores(s_b, j0 + 1)
        reduce(s_a, j0)
        scores(s_a, jnp.minimum(j0 + 2, last))
        reduce(s_b, j0 + 1)
        return c

    lax.fori_loop(0, n // 2, pair, 0)
    return [acc_scr[0:dv] / acc_scr[dv:dv + 1] for acc_scr in acc_scrs]


def _mla_attn_kernel(qt_ref, k_ref, vt_ref, mask_ref, o_ref, sa, sb, m_scr, acc_scr, *, ta):
    qt = jnp.concatenate([qt_ref[0], qt_ref[1]], axis=1)
    (o,) = _attn_maps([qt], k_ref, vt_ref, mask_ref, [sa], [sb], [m_scr], [acc_scr], ta, MLA_V)
    o_ref[...] = o.T.astype(o_ref.dtype)


def _attn_scratch(nmaps, dv, ta):
    return ([pltpu.VMEM((ta, 2 * ta), F32)] * (2 * nmaps)
            + [pltpu.VMEM((1, 2 * ta), F32), pltpu.VMEM((dv + ONES_ROWS, 2 * ta), F32)] * nmaps)


def _mla_attn(qt, k, vt, mask, batch, seq, ta):
    t = k.shape[0]
    nk = seq // ta
    nq = nk // 2
    return pl.pallas_call(
        functools.partial(_mla_attn_kernel, ta=ta),
        grid=(batch, MLA_HEADS, nq),
        in_specs=[pl.BlockSpec((None, None, 2, MLA_HEAD_PAD, ta), lambda b, h, i: (b, h, i, 0, 0)),
                  pl.BlockSpec((seq, MLA_HEAD_PAD), lambda b, h, i: (b, h)),
                  pl.BlockSpec((None, None, nk, MLA_V + ONES_ROWS, ta), lambda b, h, i: (b, h, 0, 0, 0)),
                  pl.BlockSpec(mask.shape, lambda b, h, i: (0, 0, 0))],
        out_specs=pl.BlockSpec((2 * ta, MLA_V), lambda b, h, i: (b * nq + i, h)),
        out_shape=jax.ShapeDtypeStruct((t, MLA_HEADS * MLA_V), BF16),
        scratch_shapes=_attn_scratch(1, MLA_V, ta),
        compiler_params=_cparams(("parallel", "parallel", "arbitrary")),
        name="mla_attn",
    )(qt, k, vt, mask)


def _diff_attn_kernel(qt_ref, k_ref, vt_ref, mask_ref, lq_ref, g_ref, o_ref,
                      sa1, sa2, sb1, sb2, m1, a1, m2, a2, *, ta, lambda_init):
    qt = jnp.concatenate([qt_ref[0], qt_ref[1]], axis=1)
    row = lax.broadcasted_iota(jnp.int32, qt.shape, 0)
    zero = jnp.zeros_like(qt)
    qts = [jnp.where(row < DIFF_HD, qt, zero), jnp.where(row >= DIFF_HD, qt, zero)]
    o1, o2 = _attn_maps(qts, k_ref, vt_ref, mask_ref, [sa1, sa2], [sb1, sb2],
                        [m1, m2], [a1, a2], ta, DIFF_VD)
    lq = lq_ref[...]
    lam = (jnp.exp(jnp.sum(lq[0:1] * lq[1:2], axis=1, keepdims=True))
           - jnp.exp(jnp.sum(lq[2:3] * lq[3:4], axis=1, keepdims=True)) + lambda_init)
    o = (o1 - lam * o2).T
    o_ref[...] = (_rms(o, g_ref[...]) * (1.0 - lambda_init)).astype(o_ref.dtype)


def _diff_attn(qt, k, vt, mask, lq, g, batch, seq, ta, lambda_init):
    t = k.shape[0]
    nk = seq // ta
    nq = nk // 2
    return pl.pallas_call(
        functools.partial(_diff_attn_kernel, ta=ta, lambda_init=lambda_init),
        grid=(batch, DIFF_HEADS, nq),
        in_specs=[pl.BlockSpec((None, None, 2, 128, ta), lambda b, h, i: (b, h, i, 0, 0)),
                  pl.BlockSpec((seq, 128), lambda b, h, i: (b, h)),
                  pl.BlockSpec((None, None, nk, DIFF_VD + ONES_ROWS, ta), lambda b, h, i: (b, h, 0, 0, 0)),
                  pl.BlockSpec(mask.shape, lambda b, h, i: (0, 0, 0)),
                  pl.BlockSpec(lq.shape, lambda b, h, i: (0, 0)),
                  pl.BlockSpec(g.shape, lambda b, h, i: (0, 0))],
        out_specs=pl.BlockSpec((2 * ta, DIFF_VD), lambda b, h, i: (b * nq + i, h)),
        out_shape=jax.ShapeDtypeStruct((t, DIFF_HEADS * DIFF_VD), BF16),
        scratch_shapes=_attn_scratch(2, DIFF_VD, ta),
        compiler_params=_cparams(("parallel", "parallel", "arbitrary")),
        name="diff_attn",
    )(qt, k, vt, mask, lq, g)


def _merge_kernel(om_ref, od_ref, ga_ref, gb_ref, ba_ref, bb_ref, wom_ref, wod_ref, wout_ref,
                  x_ref, gn_ref, x2_ref, ht_ref, acc_scr):
    j = pl.program_id(1)
    ya = jnp.dot(om_ref[...], wom_ref[...], preferred_element_type=F32)
    yb = jnp.dot(od_ref[...], wod_ref[...], preferred_element_type=F32)
    ga = jax.nn.sigmoid(ga_ref[...].astype(F32) + ba_ref[...])
    gb = jax.nn.sigmoid(gb_ref[...].astype(F32) + bb_ref[...])
    merged = (ga * ya + gb * yb).astype(BF16)
    contrib = jnp.dot(merged, wout_ref[...], preferred_element_type=F32)

    @pl.when(j == 0)
    def _():
        acc_scr[...] = x_ref[...] + contrib

    @pl.when(j > 0)
    def _():
        acc_scr[...] += contrib

    @pl.when(j == pl.num_programs(1) - 1)
    def _():
        x2 = acc_scr[...]
        x2_ref[...] = x2
        ht_ref[...] = _rms(x2, gn_ref[...]).T.astype(BF16)


def _merge(om, od, p, b_gate, wom, wod, wout, x2d, gn, tm):
    t, d = x2d.shape
    tn = P_TN
    nj = d // tn
    kin = om.shape[1]
    return pl.pallas_call(
        _merge_kernel,
        grid=(t // tm, nj),
        in_specs=[pl.BlockSpec((tm, kin), lambda i, j: (i, 0)),
                  pl.BlockSpec((tm, kin), lambda i, j: (i, 0)),
                  pl.BlockSpec((tm, tn), lambda i, j: (i, GATE_BLK + j)),
                  pl.BlockSpec((tm, tn), lambda i, j: (i, GATE_BLK + nj + j)),
                  pl.BlockSpec((1, tn), lambda i, j: (0, j)),
                  pl.BlockSpec((1, tn), lambda i, j: (0, nj + j)),
                  pl.BlockSpec((kin, tn), lambda i, j: (0, j)),
                  pl.BlockSpec((kin, tn), lambda i, j: (0, j)),
                  pl.BlockSpec((tn, d), lambda i, j: (j, 0)),
                  pl.BlockSpec((tm, d), lambda i, j: (i, 0)),
                  pl.BlockSpec((1, d), lambda i, j: (0, 0))],
        out_specs=[pl.BlockSpec((tm, d), lambda i, j: (i, 0)),
                   pl.BlockSpec((d, tm), lambda i, j: (0, i))],
        out_shape=[jax.ShapeDtypeStruct((t, d), F32), jax.ShapeDtypeStruct((d, t), BF16)],
        scratch_shapes=[pltpu.VMEM((tm, d), F32)],
        compiler_params=_cparams(("parallel", "arbitrary")),
        name="merge",
    )(om, od, p, p, b_gate, b_gate, wom, wod, wout, x2d, gn)


def _top16(s, iota_r, exact):
    rank = jnp.full(s.shape, RANK_OUT, F32)
    vals = []
    for it in range(PEER_TOPK):
        m = jnp.max(s, axis=0, keepdims=True)
        sel = s == m
        if exact:
            first = jnp.min(jnp.where(sel, iota_r, 999.0), axis=0, keepdims=True)
            sel = iota_r == first
        rank = jnp.where(sel, float(it), rank)
        s = jnp.where(sel, -jnp.inf, s)
        vals.append(m)
    taken = jnp.sum(jnp.where(rank < RANK_OUT, 1.0, 0.0), axis=0, keepdims=True)
    return jnp.concatenate(vals, axis=0), rank, taken


def _pair_select(v1, v2, iota_k):
    top = v1[0:1] + v2[0:1]
    cand = v1 + v2[0:1]
    cnt = jnp.zeros(v1.shape, F32)
    z = jnp.zeros(top.shape, F32)
    for _ in range(PEER_TOPK):
        m = jnp.max(cand, axis=0, keepdims=True)
        first = jnp.min(jnp.where(cand == m, iota_k, 999.0), axis=0, keepdims=True)
        sel = iota_k == first
        z = z + jnp.exp(m - top)
        cnt = jnp.where(sel, cnt + 1.0, cnt)
        taken = jnp.sum(jnp.where(sel, cnt, 0.0), axis=0, keepdims=True)
        v2n = jnp.sum(jnp.where(iota_k == taken, v2, 0.0), axis=0, keepdims=True)
        v1s = jnp.sum(jnp.where(sel, v1, 0.0), axis=0, keepdims=True)
        nxt = jnp.where(taken < float(PEER_TOPK), v1s + v2n, -jnp.inf)
        cand = jnp.where(sel, nxt, cand)
    return cnt, z


def _peer_topk_kernel(ht_ref, wqt_ref, sk_ref, cnt_ref, e1_ref, rank_ref, e2_ref, s_scr, v_scr):
    tn = ht_ref.shape[1]
    iota_k = lax.broadcasted_iota(jnp.int32, (PEER_TOPK, PEER_HEADS * tn), 0).astype(F32)

    def scores(h):
        q = jnp.dot(wqt_ref[2 * PEER_HALF * h:2 * PEER_HALF * (h + 1), :], ht_ref[...],
                    preferred_element_type=F32)
        s1 = jnp.dot(sk_ref[0], q[:PEER_HALF].astype(BF16), preferred_element_type=F32)
        s2 = jnp.dot(sk_ref[1], q[PEER_HALF:].astype(BF16), preferred_element_type=F32)
        return jnp.concatenate([s1, s2], axis=1)

    def select(h, s12, exact):
        iota_r = lax.broadcasted_iota(jnp.int32, s12.shape, 0).astype(F32) if exact else None
        v12, rank12, taken = _top16(s12, iota_r, exact)
        v_scr[h] = v12
        cnt_ref[h] = rank12[:, :tn]
        rank_ref[h] = rank12[:, tn:].astype(BF16)
        e2_ref[h] = jnp.exp(s12[:, tn:] - v12[0:1, tn:]).astype(BF16)
        return taken

    def finish():
        v1 = jnp.concatenate([v_scr[h, :, :tn] for h in range(PEER_HEADS)], axis=1)
        v2 = jnp.concatenate([v_scr[h, :, tn:] for h in range(PEER_HEADS)], axis=1)
        cnt, z = _pair_select(v1, v2, iota_k)
        for h in range(PEER_HEADS):
            lanes = slice(h * tn, (h + 1) * tn)
            rank1 = cnt_ref[h]
            cnt_keys = jnp.zeros(rank1.shape, F32)
            for i in range(PEER_TOPK):
                cnt_keys = jnp.where(rank1 == float(i), cnt[i:i + 1, lanes], cnt_keys)
            cnt_ref[h] = cnt_keys
            e1_ref[h] = jnp.exp(s_scr[h, :, :tn] - v1[0:1, lanes]) * (1.0 / z[:, lanes])

    most_taken = jnp.zeros((1, 2 * tn), F32)
    s12 = scores(0)
    for h in range(PEER_HEADS):
        s12_next = scores(h + 1) if h + 1 < PEER_HEADS else None
        s_scr[h] = s12
        most_taken = jnp.maximum(most_taken, select(h, s12, exact=False))
        s12 = s12_next
    finish()

    @pl.when(jnp.max(most_taken) > float(PEER_TOPK))
    def _():
        def head(h, c):
            select(h, s_scr[h], exact=True)
            return c

        lax.fori_loop(0, PEER_HEADS, head, 0)
        finish()


def _peer_topk(ht, wqt, sk, tn):
    d, t = ht.shape
    aux = lambda dt: jax.ShapeDtypeStruct((PEER_HEADS, PEER_NKEYS, t), dt)
    aux_spec = pl.BlockSpec((PEER_HEADS, PEER_NKEYS, tn), lambda i: (0, 0, i))
    return pl.pallas_call(
        _peer_topk_kernel,
        grid=(t // tn,),
        in_specs=[pl.BlockSpec((d, tn), lambda i: (0, i)),
                  pl.BlockSpec(wqt.shape, lambda i: (0, 0)),
                  pl.BlockSpec(sk.shape, lambda i: (0, 0, 0))],
        out_specs=[aux_spec, aux_spec, aux_spec, aux_spec],
        out_shape=[aux(F32), aux(F32), aux(BF16), aux(BF16)],
        scratch_shapes=[pltpu.VMEM((PEER_HEADS, PEER_NKEYS, 2 * tn), F32),
                        pltpu.VMEM((PEER_HEADS, PEER_TOPK, 2 * tn), F32)],
        compiler_params=_cparams(("parallel",)),
        name="peer_topk",
    )(ht, wqt, sk)


def _peer_dense_kernel(u_ref, vta_ref, vtb_ref, ht_ref, cnt_ref, e1_ref, rank_ref, e2_ref, o_ref,
                       wa_scr, wb_scr, *, rows, nj):
    j = pl.program_id(1)
    nk = PEER_NKEYS

    def gates(r):
        g = jnp.zeros((nk, ht_ref.shape[1]), BF16)
        for h in range(PEER_HEADS):
            cnt = cnt_ref[h, r:r + 1, :].astype(BF16)
            e1 = e1_ref[h, r:r + 1, :].astype(BF16)
            g = g + jnp.where(rank_ref[h] < cnt, e2_ref[h] * e1, jnp.zeros_like(g))
        return g

    half = rows // 2
    hk = half * nk

    def weights(lo):
        g = jnp.concatenate([gates(r) for r in range(lo, lo + half)], axis=0)
        at = jnp.dot(u_ref[nk * lo:nk * lo + hk, :], ht_ref[...], preferred_element_type=F32)
        return jax.nn.gelu(at).astype(BF16) * g

    @pl.when(j == 0)
    def _():
        o_ref[...] = jnp.zeros(o_ref.shape, F32)
        wb_scr[...] = jnp.zeros(wb_scr.shape, BF16)

    @pl.when(j < nj)
    def _():
        wa = weights(0)
        o_ref[...] += jnp.dot(vtb_ref[...], wb_scr[...], preferred_element_type=F32)
        wa_scr[...] = wa
        wb = weights(half)
        o_ref[...] += jnp.dot(vta_ref[...], wa_scr[...], preferred_element_type=F32)
        wb_scr[...] = wb

    @pl.when(j == nj)
    def _():
        o_ref[...] += jnp.dot(vtb_ref[...], wb_scr[...], preferred_element_type=F32)


def _peer_dense(u, vt, ht, cnt, e1, rank, e2, tn, rows):
    e, d = u.shape
    t = ht.shape[1]
    te = rows * PEER_NKEYS
    nj = e // te
    cur = lambda j: jnp.minimum(j, nj - 1)
    row_spec = pl.BlockSpec((PEER_HEADS, rows, tn), lambda i, j: (0, cur(j), i))
    col_spec = pl.BlockSpec((PEER_HEADS, PEER_NKEYS, tn), lambda i, j: (0, 0, i))
    return pl.pallas_call(
        functools.partial(_peer_dense_kernel, rows=rows, nj=nj),
        grid=(t // tn, nj + 1),
        in_specs=[pl.BlockSpec((te, d), lambda i, j: (cur(j), 0)),
                  pl.BlockSpec((d, te // 2), lambda i, j: (0, 2 * cur(j))),
                  pl.BlockSpec((d, te // 2), lambda i, j: (0, jnp.maximum(2 * j - 1, 0))),
                  pl.BlockSpec((d, tn), lambda i, j: (0, i)),
                  row_spec, row_spec, col_spec, col_spec],
        out_specs=pl.BlockSpec((d, tn), lambda i, j: (0, i)),
        out_shape=jax.ShapeDtypeStruct((d, t), F32),
        scratch_shapes=[pltpu.VMEM((te // 2, tn), BF16), pltpu.VMEM((te // 2, tn), BF16)],
        compiler_params=_cparams(("parallel", "arbitrary")),
        name="peer_dense",
    )(u, vt, vt, ht, cnt, e1, rank, e2)


def _final_kernel(x_ref, pt_ref, g_ref, o_ref):
    o_ref[...] = _rms(x_ref[...] + pt_ref[...].T, g_ref[...])


def _final(x2, pt, g, tm):
    t, d = x2.shape
    return pl.pallas_call(
        _final_kernel,
        grid=(t // tm,),
        in_specs=[pl.BlockSpec((tm, d), lambda i: (i, 0)),
                  pl.BlockSpec((d, tm), lambda i: (0, i)),
                  pl.BlockSpec((1, d), lambda i: (0, 0))],
        out_specs=pl.BlockSpec((tm, d), lambda i: (i, 0)),
        out_shape=jax.ShapeDtypeStruct((t, d), F32),
        compiler_params=_cparams(("parallel",)),
        name="final",
    )(x2, pt, g)


def _rope_tables(seq):
    half = MLA_ROPE // 2
    inv = ROPE_THETA ** (-jnp.arange(0, MLA_ROPE, 2, dtype=F32) / MLA_ROPE)
    ang = jnp.arange(seq, dtype=F32)[:, None] * inv[None, :]
    cos = jnp.tile(jnp.cos(ang), (1, 128 // half))
    sin = jnp.tile(jnp.sin(ang), (1, 128 // half))
    first_half = (jnp.arange(128) % MLA_ROPE) < half
    return cos, jnp.where(first_half, -sin, 0.0), jnp.where(first_half, 0.0, sin)


def _attn_mask(ta):
    key = jnp.arange(2 * ta).reshape(2, ta, 1)
    qry = jnp.arange(2 * ta).reshape(1, 1, 2 * ta)
    return jnp.where(key <= qry, 0.0, NEG).astype(F32)


def _tiles(seq):
    pick = lambda pref: max(c for c in (128, 256, 512, 1024) if c <= pref and seq % c == 0)
    attn = max(c for c in (128, 256, 512) if seq % (2 * c) == 0)
    return dict(proj=pick(1024), attn=attn, merge=pick(512),
                topk=pick(256), dense=pick(512), final=pick(512))


def kernel(x, w_in, b_gate, g_norm1, g_cq, w_uq, g_ckv, w_ukv, w_o_mla, lambda_qk, g_subln,
           w_o_diff, w_out, g_norm2, w_q_peer, sub_keys, expert_u, expert_v, g_final):
    batch, seq, d = x.shape
    assert d == D_MODEL and seq % 256 == 0
    assert w_in.shape[0] == 1, "single-layer trunk"
    l = 0
    lambda_init = 0.8 - 0.6 * math.exp(-0.3 * l)
    tl = _tiles(seq)
    ta = tl["attn"]
    cos, sl, sh = _rope_tables(seq)
    mask = _attn_mask(ta)
    x2d = x.reshape(batch * seq, d)
    row = lambda v: v.reshape(1, -1)

    wl = w_in[l]
    zpad = jnp.zeros((d, P_MLA - (MLA_Q_RANK + MLA_KV_RANK + MLA_ROPE)), F32)
    w_p = jnp.concatenate([wl[:, OFF_DQ:OFF_GATE], wl[:, OFF_CQ:OFF_DQ], zpad, wl[:, OFF_GATE:]],
                          axis=1).astype(BF16)
    wuq = w_uq[l].reshape(MLA_Q_RANK, MLA_HEADS, MLA_QK)
    wuq = jnp.pad(wuq, ((0, 0), (0, 0), (0, MLA_HEAD_PAD - MLA_QK)))
    wuq = wuq.reshape(MLA_Q_RANK, MLA_HEADS * MLA_HEAD_PAD).astype(BF16)
    wukv = w_ukv[l].reshape(MLA_KV_RANK, MLA_HEADS, MLA_NOPE + MLA_V)
    wuk = wukv[:, :, :MLA_NOPE].reshape(MLA_KV_RANK, MLA_HEADS * MLA_NOPE).astype(BF16)
    wuv = wukv[:, :, MLA_NOPE:].reshape(MLA_KV_RANK, MLA_HEADS * MLA_V).astype(BF16)

    p = _proj(x2d, row(g_norm1[l]), w_p, tl["proj"])
    qdt, kd, vdt = _diff_prep(p, cos, sl, sh, batch, seq, ta)
    qmt, km, vmt = _mla_prep(p, row(g_cq[l]), row(g_ckv[l]), wuq, wuk, wuv, cos, sl, sh, batch, seq, ta)
    om = _mla_attn(qmt, km, vmt, mask, batch, seq, ta)
    od = _diff_attn(qdt, kd, vdt, mask, lambda_qk[l], row(g_subln[l]), batch, seq, ta, lambda_init)
    x2, ht = _merge(om, od, p, row(b_gate[l]), w_o_mla[l].astype(BF16), w_o_diff[l].astype(BF16),
                    w_out[l].astype(BF16), x2d, row(g_norm2[l]), tl["merge"])
    cnt, e1, rank, e2 = _peer_topk(ht, w_q_peer[l].T.astype(BF16), sub_keys[l].astype(BF16), tl["topk"])
    pt = _peer_dense(expert_u[l].astype(BF16), expert_v[l].T.astype(BF16), ht, cnt, e1, rank, e2,
                     tl["dense"], 8)
    out = _final(x2, pt, row(g_final), tl["final"])
    return out.reshape(batch, seq, d)
```

```python
import functools
import math

import jax
import jax.numpy as jnp
from jax import lax
from jax.experimental import pallas as pl
from jax.experimental.pallas import tpu as pltpu

F32 = jnp.float32
BF16 = jnp.bfloat16

EPS = 1e-6
ROPE_THETA = 10000.0
D_MODEL = 2048
LOG2E = 1.4426950408889634

MLA_HEADS = 8
MLA_Q_RANK = 768
MLA_KV_RANK = 512
MLA_NOPE = 128
MLA_ROPE = 64
MLA_V = 128
MLA_QK = MLA_NOPE + MLA_ROPE
MLA_HEAD_PAD = 256

DIFF_HEADS = 8
DIFF_HD = 64
DIFF_VD = 2 * DIFF_HD

PEER_HEADS = 8
PEER_NKEYS = 128
PEER_EXPERTS = PEER_NKEYS * PEER_NKEYS
PEER_HALF = 128
PEER_TOPK = 16
RANK_OUT = 64.0
TAKEN_BASE = 2.0 ** 100
TAKEN_STEP = 2.0 ** 77

OFF_CQ = 0
OFF_CKV = OFF_CQ + MLA_Q_RANK
OFF_KR = OFF_CKV + MLA_KV_RANK
OFF_DQ = OFF_KR + MLA_ROPE
OFF_DK = OFF_DQ + DIFF_HEADS * 2 * DIFF_HD
OFF_DV = OFF_DK + DIFF_HEADS * 2 * DIFF_HD
OFF_GATE = OFF_DV + DIFF_HEADS * DIFF_VD

P_DIFF = 3 * 1024
P_MLA = 1536
P_COLS = P_DIFF + P_MLA + 2 * D_MODEL
P_TN = 512
MLA_BLK = P_DIFF // P_MLA
GATE_BLK = (P_DIFF + P_MLA) // P_TN

NEG = -1e30
ONES_ROWS = 16
VMEM_LIMIT = 56 * 1024 * 1024


def _cparams(sem):
    return pltpu.CompilerParams(dimension_semantics=sem, vmem_limit_bytes=VMEM_LIMIT)


def _rms(x, g):
    return x * lax.rsqrt(jnp.mean(x * x, axis=-1, keepdims=True) + EPS) * g


def _rope128(x, cos, sin_lo, sin_hi):
    return x * cos + pltpu.roll(x, 96, 1) * sin_lo + pltpu.roll(x, 32, 1) * sin_hi


def _proj_kernel(x_ref, g_ref, w_ref, o_ref, h_scr):
    @pl.when(pl.program_id(1) == 0)
    def _():
        h_scr[...] = _rms(x_ref[...], g_ref[...]).astype(BF16)

    o_ref[...] = jnp.dot(h_scr[...], w_ref[...], preferred_element_type=F32).astype(o_ref.dtype)


def _proj(x2d, g, w, tm):
    t, d = x2d.shape
    n = w.shape[1]
    return pl.pallas_call(
        _proj_kernel,
        grid=(t // tm, n // P_TN),
        in_specs=[pl.BlockSpec((tm, d), lambda i, j: (i, 0)),
                  pl.BlockSpec((1, d), lambda i, j: (0, 0)),
                  pl.BlockSpec((d, P_TN), lambda i, j: (0, j))],
        out_specs=pl.BlockSpec((tm, P_TN), lambda i, j: (i, j)),
        out_shape=jax.ShapeDtypeStruct((t, n), BF16),
        scratch_shapes=[pltpu.VMEM((tm, d), BF16)],
        compiler_params=_cparams(("parallel", "arbitrary")),
        name="proj",
    )(x2d, g, w)


def _diff_prep_kernel(q_ref, k_ref, v_ref, cos_ref, sl_ref, sh_ref, qt_ref, ko_ref, vt_ref, *, scale):
    cos, sl, sh = cos_ref[...], sl_ref[...], sh_ref[...]
    for h in range(DIFF_HEADS):
        c = slice(128 * h, 128 * h + 128)
        q = _rope128(q_ref[:, c].astype(F32), cos, sl, sh) * scale
        qt_ref[h] = q.T.astype(BF16)
        ko_ref[:, c] = _rope128(k_ref[:, c].astype(F32), cos, sl, sh).astype(BF16)
        vt_ref[h, 0:DIFF_VD, :] = v_ref[:, c].astype(F32).T.astype(BF16)
        vt_ref[h, DIFF_VD:DIFF_VD + ONES_ROWS, :] = jnp.ones((ONES_ROWS, q.shape[0]), BF16)


def _diff_prep(p, cos, sl, sh, batch, seq, ta):
    t = p.shape[0]
    w = DIFF_HEADS * 2 * DIFF_HD
    ns = seq // ta
    tab = pl.BlockSpec((ta, 128), lambda i: (i % ns, 0))
    tr = lambda rows: (jax.ShapeDtypeStruct((batch, DIFF_HEADS, ns, rows, ta), BF16),
                       pl.BlockSpec((None, DIFF_HEADS, None, rows, ta), lambda i: (i // ns, 0, i % ns, 0, 0)))
    qt_shape, qt_spec = tr(2 * DIFF_HD)
    vt_shape, vt_spec = tr(DIFF_VD + ONES_ROWS)
    return pl.pallas_call(
        functools.partial(_diff_prep_kernel, scale=DIFF_HD ** -0.5 * LOG2E),
        grid=(t // ta,),
        in_specs=[pl.BlockSpec((ta, w), lambda i: (i, 0)),
                  pl.BlockSpec((ta, w), lambda i: (i, 1)),
                  pl.BlockSpec((ta, w), lambda i: (i, 2)),
                  tab, tab, tab],
        out_specs=[qt_spec, pl.BlockSpec((ta, w), lambda i: (i, 0)), vt_spec],
        out_shape=[qt_shape, jax.ShapeDtypeStruct((t, w), BF16), vt_shape],
        compiler_params=_cparams(("parallel",)),
        name="diff_prep",
    )(p, p, p, cos, sl, sh)


def _mla_prep_kernel(p_ref, gq_ref, gkv_ref, wuq_ref, wuk_ref, wuv_ref, cos_ref, sl_ref, sh_ref,
                     qt_ref, k_ref, vt_ref, *, scale):
    cos, sl, sh = cos_ref[...], sl_ref[...], sh_ref[...]
    cq = _rms(p_ref[:, 0:MLA_Q_RANK].astype(F32), gq_ref[...]).astype(BF16)
    yq = jnp.dot(cq, wuq_ref[...], preferred_element_type=F32)
    ckv = _rms(p_ref[:, MLA_Q_RANK:MLA_Q_RANK + MLA_KV_RANK].astype(F32), gkv_ref[...]).astype(BF16)
    kn = jnp.dot(ckv, wuk_ref[...], preferred_element_type=F32)
    vv = jnp.dot(ckv, wuv_ref[...], preferred_element_type=F32)
    kr_off = MLA_Q_RANK + MLA_KV_RANK
    kr = _rope128(p_ref[:, kr_off:kr_off + 128].astype(F32), cos, sl, sh).astype(BF16)
    for h in range(MLA_HEADS):
        a = MLA_HEAD_PAD * h
        qt_ref[h, 0:128, :] = (yq[:, a:a + 128] * scale).T.astype(BF16)
        qt_ref[h, 128:256, :] = (_rope128(yq[:, a + 128:a + 256], cos, sl, sh) * scale).T.astype(BF16)
        k_ref[:, a:a + 128] = kn[:, 128 * h:128 * h + 128].astype(BF16)
        k_ref[:, a + 128:a + 256] = kr
        vt_ref[h, 0:MLA_V, :] = vv[:, 128 * h:128 * h + 128].T.astype(BF16)
        vt_ref[h, MLA_V:MLA_V + ONES_ROWS, :] = jnp.ones((ONES_ROWS, vv.shape[0]), BF16)


def _mla_prep(p, gq, gkv, wuq, wuk, wuv, cos, sl, sh, batch, seq, ta):
    t = p.shape[0]
    ns = seq // ta
    tab = pl.BlockSpec((ta, 128), lambda i: (i % ns, 0))
    full = lambda a: pl.BlockSpec(a.shape, lambda i: (0, 0))
    hw = MLA_HEADS * MLA_HEAD_PAD
    tr = lambda rows: (jax.ShapeDtypeStruct((batch, MLA_HEADS, ns, rows, ta), BF16),
                       pl.BlockSpec((None, MLA_HEADS, None, rows, ta), lambda i: (i // ns, 0, i % ns, 0, 0)))
    qt_shape, qt_spec = tr(MLA_HEAD_PAD)
    vt_shape, vt_spec = tr(MLA_V + ONES_ROWS)
    return pl.pallas_call(
        functools.partial(_mla_prep_kernel, scale=MLA_QK ** -0.5 * LOG2E),
        grid=(t // ta,),
        in_specs=[pl.BlockSpec((ta, P_MLA), lambda i: (i, MLA_BLK)),
                  full(gq), full(gkv), full(wuq), full(wuk), full(wuv), tab, tab, tab],
        out_specs=[qt_spec, pl.BlockSpec((ta, hw), lambda i: (i, 0)), vt_spec],
        out_shape=[qt_shape, jax.ShapeDtypeStruct((t, hw), BF16), vt_shape],
        compiler_params=_cparams(("parallel",)),
        name="mla_prep",
    )(p, gq, gkv, wuq, wuk, wuv, cos, sl, sh)


def _attn_maps(qts, k_ref, vt_ref, mask_ref, s_a, s_b, m_scrs, acc_scrs, ta, dv):
    n = 2 * pl.program_id(2)

    def scores(bufs, j):
        start = pl.multiple_of(j * ta, ta)
        kj = k_ref[pl.ds(start, ta), :]
        for buf, qt in zip(bufs, qts):
            buf[...] = jnp.dot(kj, qt, preferred_element_type=F32)

    def reduce(bufs, j, mask=None, first=False):
        vt = vt_ref[j]
        for buf, m_scr, acc_scr in zip(bufs, m_scrs, acc_scrs):
            s = buf[...] if mask is None else buf[...] + mask_ref[mask]
            if first:
                m_new = jnp.max(s, axis=0, keepdims=True)
                p = jnp.exp2(s - m_new).astype(BF16)
                acc_scr[...] = jnp.dot(vt, p, preferred_element_type=F32)
            else:
                m_old = m_scr[...]
                m_new = jnp.maximum(m_old, jnp.max(s, axis=0, keepdims=True))
                p = jnp.exp2(s - m_new).astype(BF16)
                acc_scr[...] = (jnp.exp2(m_old - m_new) * acc_scr[...]
                                + jnp.dot(vt, p, preferred_element_type=F32))
            m_scr[...] = m_new

    last = jnp.maximum(n - 1, 0)
    scores(s_a, n)
    scores(s_b, n + 1)
    reduce(s_a, n, mask=0, first=True)
    scores(s_a, 0)
    reduce(s_b, n + 1, mask=1)

    def pair(pp, c):
        j0 = 2 * pp
        scores(s_b, j0 + 1)
        reduce(s_a, j0)
        scores(s_a, jnp.minimum(j0 + 2, last))
        reduce(s_b, j0 + 1)
        return c

    lax.fori_loop(0, n // 2, pair, 0)
    return [acc_scr[0:dv] / acc_scr[dv:dv + 1] for acc_scr in acc_scrs]


def _mla_attn_kernel(qt_ref, k_ref, vt_ref, mask_ref, o_ref, sa, sb, m_scr, acc_scr, *, ta):
    qt = jnp.concatenate([qt_ref[0], qt_ref[1]], axis=1)
    (o,) = _attn_maps([qt], k_ref, vt_ref, mask_ref, [sa], [sb], [m_scr], [acc_scr], ta, MLA_V)
    o_ref[...] = o.T.astype(o_ref.dtype)


def _attn_scratch(nmaps, dv, ta):
    return ([pltpu.VMEM((ta, 2 * ta), F32)] * (2 * nmaps)
            + [pltpu.VMEM((1, 2 * ta), F32), pltpu.VMEM((dv + ONES_ROWS, 2 * ta), F32)] * nmaps)


def _mla_attn(qt, k, vt, mask, batch, seq, ta):
    t = k.shape[0]
    nk = seq // ta
    nq = nk // 2
    return pl.pallas_call(
        functools.partial(_mla_attn_kernel, ta=ta),
        grid=(batch, MLA_HEADS, nq),
        in_specs=[pl.BlockSpec((None, None, 2, MLA_HEAD_PAD, ta), lambda b, h, i: (b, h, i, 0, 0)),
                  pl.BlockSpec((seq, MLA_HEAD_PAD), lambda b, h, i: (b, h)),
                  pl.BlockSpec((None, None, nk, MLA_V + ONES_ROWS, ta), lambda b, h, i: (b, h, 0, 0, 0)),
                  pl.BlockSpec(mask.shape, lambda b, h, i: (0, 0, 0))],
        out_specs=pl.BlockSpec((2 * ta, MLA_V), lambda b, h, i: (b * nq + i, h)),
        out_shape=jax.ShapeDtypeStruct((t, MLA_HEADS * MLA_V), BF16),
        scratch_shapes=_attn_scratch(1, MLA_V, ta),
        compiler_params=_cparams(("parallel", "parallel", "arbitrary")),
        name="mla_attn",
    )(qt, k, vt, mask)


def _diff_attn_kernel(qt_ref, k_ref, vt_ref, mask_ref, lq_ref, g_ref, o_ref,
                      sa1, sa2, sb1, sb2, m1, a1, m2, a2, *, ta, lambda_init):
    qt = jnp.concatenate([qt_ref[0], qt_ref[1]], axis=1)
    row = lax.broadcasted_iota(jnp.int32, qt.shape, 0)
    zero = jnp.zeros_like(qt)
    qts = [jnp.where(row < DIFF_HD, qt, zero), jnp.where(row >= DIFF_HD, qt, zero)]
    o1, o2 = _attn_maps(qts, k_ref, vt_ref, mask_ref, [sa1, sa2], [sb1, sb2],
                        [m1, m2], [a1, a2], ta, DIFF_VD)
    lq = lq_ref[...]
    lam = (jnp.exp(jnp.sum(lq[0:1] * lq[1:2], axis=1, keepdims=True))
           - jnp.exp(jnp.sum(lq[2:3] * lq[3:4], axis=1, keepdims=True)) + lambda_init)
    o = (o1 - lam * o2).T
    o_ref[...] = (_rms(o, g_ref[...]) * (1.0 - lambda_init)).astype(o_ref.dtype)


def _diff_attn(qt, k, vt, mask, lq, g, batch, seq, ta, lambda_init):
    t = k.shape[0]
    nk = seq // ta
    nq = nk // 2
    return pl.pallas_call(
        functools.partial(_diff_attn_kernel, ta=ta, lambda_init=lambda_init),
        grid=(batch, DIFF_HEADS, nq),
        in_specs=[pl.BlockSpec((None, None, 2, 128, ta), lambda b, h, i: (b, h, i, 0, 0)),
                  pl.BlockSpec((seq, 128), lambda b, h, i: (b, h)),
                  pl.BlockSpec((None, None, nk, DIFF_VD + ONES_ROWS, ta), lambda b, h, i: (b, h, 0, 0, 0)),
                  pl.BlockSpec(mask.shape, lambda b, h, i: (0, 0, 0)),
                  pl.BlockSpec(lq.shape, lambda b, h, i: (0, 0)),
                  pl.BlockSpec(g.shape, lambda b, h, i: (0, 0))],
        out_specs=pl.BlockSpec((2 * ta, DIFF_VD), lambda b, h, i: (b * nq + i, h)),
        out_shape=jax.ShapeDtypeStruct((t, DIFF_HEADS * DIFF_VD), BF16),
        scratch_shapes=_attn_scratch(2, DIFF_VD, ta),
        compiler_params=_cparams(("parallel", "parallel", "arbitrary")),
        name="diff_attn",
    )(qt, k, vt, mask, lq, g)


def _merge_kernel(om_ref, od_ref, ga_ref, gb_ref, ba_ref, bb_ref, wom_ref, wod_ref, wout_ref,
                  x_ref, gn_ref, x2_ref, ht_ref, acc_scr):
    j = pl.program_id(1)

    @pl.when(j == 0)
    def _():
        acc_scr[...] = x_ref[...]

    ya = jnp.dot(om_ref[...], wom_ref[...], preferred_element_type=F32)
    yb = jnp.dot(od_ref[...], wod_ref[...], preferred_element_type=F32)
    ga = jax.nn.sigmoid(ga_ref[...].astype(F32) + ba_ref[...])
    gb = jax.nn.sigmoid(gb_ref[...].astype(F32) + bb_ref[...])
    merged = (ga * ya + gb * yb).astype(BF16)
    acc_scr[...] += jnp.dot(merged, wout_ref[...], preferred_element_type=F32)

    @pl.when(j == pl.num_programs(1) - 1)
    def _():
        x2 = acc_scr[...]
        x2_ref[...] = x2
        ht_ref[...] = _rms(x2, gn_ref[...]).T.astype(BF16)


def _merge(om, od, p, b_gate, wom, wod, wout, x2d, gn, tm):
    t, d = x2d.shape
    tn = P_TN
    nj = d // tn
    kin = om.shape[1]
    return pl.pallas_call(
        _merge_kernel,
        grid=(t // tm, nj),
        in_specs=[pl.BlockSpec((tm, kin), lambda i, j: (i, 0)),
                  pl.BlockSpec((tm, kin), lambda i, j: (i, 0)),
                  pl.BlockSpec((tm, tn), lambda i, j: (i, GATE_BLK + j)),
                  pl.BlockSpec((tm, tn), lambda i, j: (i, GATE_BLK + nj + j)),
                  pl.BlockSpec((1, tn), lambda i, j: (0, j)),
                  pl.BlockSpec((1, tn), lambda i, j: (0, nj + j)),
                  pl.BlockSpec((kin, tn), lambda i, j: (0, j)),
                  pl.BlockSpec((kin, tn), lambda i, j: (0, j)),
                  pl.BlockSpec((tn, d), lambda i, j: (j, 0)),
                  pl.BlockSpec((tm, d), lambda i, j: (i, 0)),
                  pl.BlockSpec((1, d), lambda i, j: (0, 0))],
        out_specs=[pl.BlockSpec((tm, d), lambda i, j: (i, 0)),
                   pl.BlockSpec((d, tm), lambda i, j: (0, i))],
        out_shape=[jax.ShapeDtypeStruct((t, d), F32), jax.ShapeDtypeStruct((d, t), BF16)],
        scratch_shapes=[pltpu.VMEM((tm, d), F32)],
        compiler_params=_cparams(("parallel", "arbitrary")),
        name="merge",
    )(om, od, p, p, b_gate, b_gate, wom, wod, wout, x2d, gn)


def _top16(s, iota_r, exact):
    vals = []
    for it in range(PEER_TOPK):
        m = jnp.max(s, axis=0, keepdims=True)
        sel = s == m
        if exact:
            first = jnp.min(jnp.where(sel, iota_r, 999.0), axis=0, keepdims=True)
            sel = iota_r == first
        s = jnp.where(sel, -(TAKEN_BASE + it * TAKEN_STEP), s)
        vals.append(m)
    was_taken = s <= -TAKEN_BASE
    rank = jnp.where(was_taken, (-s - TAKEN_BASE) * (1.0 / TAKEN_STEP), RANK_OUT)
    taken = jnp.sum(jnp.where(was_taken, 1.0, 0.0), axis=0, keepdims=True)
    return jnp.concatenate(vals, axis=0), rank, taken


def _pair_select(v1, v2, iota_k):
    top = v1[0:1] + v2[0:1]
    cand = v1 + v2[0:1]
    cnt = jnp.zeros(v1.shape, F32)
    z = jnp.zeros(top.shape, F32)
    for _ in range(PEER_TOPK):
        m = jnp.max(cand, axis=0, keepdims=True)
        first = jnp.min(jnp.where(cand == m, iota_k, 999.0), axis=0, keepdims=True)
        sel = iota_k == first
        z = z + jnp.exp(m - top)
        cnt = jnp.where(sel, cnt + 1.0, cnt)
        taken = jnp.sum(jnp.where(sel, cnt, 0.0), axis=0, keepdims=True)
        v2n = jnp.sum(jnp.where(iota_k == taken, v2, 0.0), axis=0, keepdims=True)
        v1s = jnp.sum(jnp.where(sel, v1, 0.0), axis=0, keepdims=True)
        nxt = jnp.where(taken < float(PEER_TOPK), v1s + v2n, -jnp.inf)
        cand = jnp.where(sel, nxt, cand)
    return cnt, z


def _peer_topk_kernel(ht_ref, wqt_ref, sk_ref, cnt_ref, e1_ref, rank_ref, e2_ref, s_scr, v_scr):
    tn = ht_ref.shape[1]
    iota_k = lax.broadcasted_iota(jnp.int32, (PEER_TOPK, PEER_HEADS * tn), 0).astype(F32)

    def scores(h):
        q = jnp.dot(wqt_ref[2 * PEER_HALF * h:2 * PEER_HALF * (h + 1), :], ht_ref[...],
                    preferred_element_type=F32)
        s1 = jnp.dot(sk_ref[0], q[:PEER_HALF].astype(BF16), preferred_element_type=F32)
        s2 = jnp.dot(sk_ref[1], q[PEER_HALF:].astype(BF16), preferred_element_type=F32)
        return jnp.concatenate([s1, s2], axis=1)

    def select(h, s12, exact):
        iota_r = lax.broadcasted_iota(jnp.int32, s12.shape, 0).astype(F32) if exact else None
        v12, rank12, taken = _top16(s12, iota_r, exact)
        v_scr[h] = v12
        cnt_ref[h] = rank12[:, :tn]
        rank_ref[h] = rank12[:, tn:].astype(BF16)
        e2_ref[h] = jnp.exp(s12[:, tn:] - v12[0:1, tn:]).astype(BF16)
        return taken

    def finish():
        v1 = jnp.concatenate([v_scr[h, :, :tn] for h in range(PEER_HEADS)], axis=1)
        v2 = jnp.concatenate([v_scr[h, :, tn:] for h in range(PEER_HEADS)], axis=1)
        cnt, z = _pair_select(v1, v2, iota_k)
        for h in range(PEER_HEADS):
            lanes = slice(h * tn, (h + 1) * tn)
            rank1 = cnt_ref[h]
            cnt_keys = jnp.zeros(rank1.shape, F32)
            for i in range(PEER_TOPK):
                cnt_keys = jnp.where(rank1 == float(i), cnt[i:i + 1, lanes], cnt_keys)
            cnt_ref[h] = cnt_keys
            e1_ref[h] = jnp.exp(s_scr[h, :, :tn] - v1[0:1, lanes]) * (1.0 / z[:, lanes])

    most_taken = jnp.zeros((1, 2 * tn), F32)
    s12 = scores(0)
    for h in range(PEER_HEADS):
        s12_next = scores(h + 1) if h + 1 < PEER_HEADS else None
        s_scr[h] = s12
        most_taken = jnp.maximum(most_taken, select(h, s12, exact=False))
        s12 = s12_next
    finish()

    @pl.when(jnp.max(most_taken) > float(PEER_TOPK))
    def _():
        def head(h, c):
            select(h, s_scr[h], exact=True)
            return c

        lax.fori_loop(0, PEER_HEADS, head, 0)
        finish()


def _peer_topk(ht, wqt, sk, tn):
    d, t = ht.shape
    aux = lambda dt: jax.ShapeDtypeStruct((PEER_HEADS, PEER_NKEYS, t), dt)
    aux_spec = pl.BlockSpec((PEER_HEADS, PEER_NKEYS, tn), lambda i: (0, 0, i))
    return pl.pallas_call(
        _peer_topk_kernel,
        grid=(t // tn,),
        in_specs=[pl.BlockSpec((d, tn), lambda i: (0, i)),
                  pl.BlockSpec(wqt.shape, lambda i: (0, 0)),
                  pl.BlockSpec(sk.shape, lambda i: (0, 0, 0))],
        out_specs=[aux_spec, aux_spec, aux_spec, aux_spec],
        out_shape=[aux(F32), aux(F32), aux(BF16), aux(BF16)],
        scratch_shapes=[pltpu.VMEM((PEER_HEADS, PEER_NKEYS, 2 * tn), F32),
                        pltpu.VMEM((PEER_HEADS, PEER_TOPK, 2 * tn), F32)],
        compiler_params=_cparams(("parallel",)),
        name="peer_topk",
    )(ht, wqt, sk)


def _peer_dense_kernel(u_ref, vta_ref, vtb_ref, ht_ref, cnt_ref, e1_ref, rank_ref, e2_ref, x_ref, gf_ref,
                       o_ref, acc_scr, wa_scr, wb_scr, *, rows, nj):
    j = pl.program_id(1)
    nk = PEER_NKEYS

    def gates(r):
        g = None
        for h in range(PEER_HEADS):
            cnt = cnt_ref[h, r:r + 1, :].astype(BF16)
            e1 = e1_ref[h, r:r + 1, :].astype(BF16)
            prod = e2_ref[h] * e1
            term = jnp.where(rank_ref[h] < cnt, prod, jnp.zeros_like(prod))
            g = term if g is None else g + term
        return g

    half = rows // 2
    hk = half * nk

    def weights(lo):
        g = jnp.concatenate([gates(r) for r in range(lo, lo + half)], axis=0)
        at = jnp.dot(u_ref[nk * lo:nk * lo + hk, :], ht_ref[...], preferred_element_type=F32)
        return jax.nn.gelu(at).astype(BF16) * g

    @pl.when(j == 0)
    def _():
        acc_scr[...] = jnp.zeros(acc_scr.shape, F32)
        wb_scr[...] = jnp.zeros(wb_scr.shape, BF16)

    @pl.when(j < nj)
    def _():
        wa = weights(0)
        acc_scr[...] += jnp.dot(vtb_ref[...], wb_scr[...], preferred_element_type=F32)
        wa_scr[...] = wa
        wb = weights(half)
        acc_scr[...] += jnp.dot(vta_ref[...], wa_scr[...], preferred_element_type=F32)
        wb_scr[...] = wb

    @pl.when(j == nj)
    def _():
        peer_t = acc_scr[...] + jnp.dot(vtb_ref[...], wb_scr[...], preferred_element_type=F32)
        o_ref[...] = _rms(x_ref[...] + peer_t.T, gf_ref[...])


def _peer_dense(u, vt, ht, cnt, e1, rank, e2, x2, g_final, tn, rows):
    e, d = u.shape
    t = ht.shape[1]
    te = rows * PEER_NKEYS
    nj = e // te
    cur = lambda j: jnp.minimum(j, nj - 1)
    row_spec = pl.BlockSpec((PEER_HEADS, rows, tn), lambda i, j: (0, cur(j), i))
    col_spec = pl.BlockSpec((PEER_HEADS, PEER_NKEYS, tn), lambda i, j: (0, 0, i))
    return pl.pallas_call(
        functools.partial(_peer_dense_kernel, rows=rows, nj=nj),
        grid=(t // tn, nj + 1),
        in_specs=[pl.BlockSpec((te, d), lambda i, j: (cur(j), 0)),
                  pl.BlockSpec((d, te // 2), lambda i, j: (0, 2 * cur(j))),
                  pl.BlockSpec((d, te // 2), lambda i, j: (0, jnp.maximum(2 * j - 1, 0))),
                  pl.BlockSpec((d, tn), lambda i, j: (0, i)),
                  row_spec, row_spec, col_spec, col_spec,
                  pl.BlockSpec((tn, d), lambda i, j: (i, 0)),
                  pl.BlockSpec((1, d), lambda i, j: (0, 0))],
        out_specs=pl.BlockSpec((tn, d), lambda i, j: (i, 0)),
        out_shape=jax.ShapeDtypeStruct((t, d), F32),
        scratch_shapes=[pltpu.VMEM((d, tn), F32),
                        pltpu.VMEM((te // 2, tn), BF16), pltpu.VMEM((te // 2, tn), BF16)],
        compiler_params=_cparams(("parallel", "arbitrary")),
        name="peer_dense",
    )(u, vt, vt, ht, cnt, e1, rank, e2, x2, g_final)


def _rope_tables(seq):
    half = MLA_ROPE // 2
    inv = ROPE_THETA ** (-jnp.arange(0, MLA_ROPE, 2, dtype=F32) / MLA_ROPE)
    ang = jnp.arange(seq, dtype=F32)[:, None] * inv[None, :]
    cos = jnp.tile(jnp.cos(ang), (1, 128 // half))
    sin = jnp.tile(jnp.sin(ang), (1, 128 // half))
    first_half = (jnp.arange(128) % MLA_ROPE) < half
    return cos, jnp.where(first_half, -sin, 0.0), jnp.where(first_half, 0.0, sin)


def _attn_mask(ta):
    key = jnp.arange(2 * ta).reshape(2, ta, 1)
    qry = jnp.arange(2 * ta).reshape(1, 1, 2 * ta)
    return jnp.where(key <= qry, 0.0, NEG).astype(F32)


def _tiles(seq):
    pick = lambda pref: max(c for c in (128, 256, 512, 1024) if c <= pref and seq % c == 0)
    attn = max(c for c in (128, 256, 512) if seq % (2 * c) == 0)
    return dict(proj=pick(1024), attn=attn, merge=pick(512),
                topk=pick(256), dense=pick(512))


def kernel(x, w_in, b_gate, g_norm1, g_cq, w_uq, g_ckv, w_ukv, w_o_mla, lambda_qk, g_subln,
           w_o_diff, w_out, g_norm2, w_q_peer, sub_keys, expert_u, expert_v, g_final):
    batch, seq, d = x.shape
    assert d == D_MODEL and seq % 256 == 0
    assert w_in.shape[0] == 1, "single-layer trunk"
    l = 0
    lambda_init = 0.8 - 0.6 * math.exp(-0.3 * l)
    tl = _tiles(seq)
    ta = tl["attn"]
    cos, sl, sh = _rope_tables(seq)
    mask = _attn_mask(ta)
    x2d = x.reshape(batch * seq, d)
    row = lambda v: v.reshape(1, -1)

    wl = w_in[l].astype(BF16)
    zpad = jnp.zeros((d, P_MLA - (MLA_Q_RANK + MLA_KV_RANK + MLA_ROPE)), BF16)
    w_p = jnp.concatenate([wl[:, OFF_DQ:OFF_GATE], wl[:, OFF_CQ:OFF_DQ], zpad, wl[:, OFF_GATE:]], axis=1)
    wuq = w_uq[l].reshape(MLA_Q_RANK, MLA_HEADS, MLA_QK)
    wuq = jnp.pad(wuq, ((0, 0), (0, 0), (0, MLA_HEAD_PAD - MLA_QK)))
    wuq = wuq.reshape(MLA_Q_RANK, MLA_HEADS * MLA_HEAD_PAD).astype(BF16)
    wukv = w_ukv[l].reshape(MLA_KV_RANK, MLA_HEADS, MLA_NOPE + MLA_V)
    wuk = wukv[:, :, :MLA_NOPE].reshape(MLA_KV_RANK, MLA_HEADS * MLA_NOPE).astype(BF16)
    wuv = wukv[:, :, MLA_NOPE:].reshape(MLA_KV_RANK, MLA_HEADS * MLA_V).astype(BF16)

    p = _proj(x2d, row(g_norm1[l]), w_p, tl["proj"])
    qdt, kd, vdt = _diff_prep(p, cos, sl, sh, batch, seq, ta)
    qmt, km, vmt = _mla_prep(p, row(g_cq[l]), row(g_ckv[l]), wuq, wuk, wuv, cos, sl, sh, batch, seq, ta)
    om = _mla_attn(qmt, km, vmt, mask, batch, seq, ta)
    od = _diff_attn(qdt, kd, vdt, mask, lambda_qk[l], row(g_subln[l]), batch, seq, ta, lambda_init)
    x2, ht = _merge(om, od, p, row(b_gate[l]), w_o_mla[l].astype(BF16), w_o_diff[l].astype(BF16),
                    w_out[l].astype(BF16), x2d, row(g_norm2[l]), tl["merge"])
    cnt, e1, rank, e2 = _peer_topk(ht, w_q_peer[l].T.astype(BF16), sub_keys[l].astype(BF16), tl["topk"])
    out = _peer_dense(expert_u[l].astype(BF16), expert_v[l].T.astype(BF16), ht, cnt, e1, rank, e2,
                      x2, row(g_final), tl["dense"], 8)
    return out.reshape(batch, seq, d)
```
